```python
import math
import jax, jax.numpy as jnp
from jax import lax
import numpy as np

D_MODEL = 1024
BATCH = 32
SEQ = 2048
DEPTH = 1
DEC_BATCH = 16
DEC_SEQ = 2048
PAST_LEN = 128

GRID_W = 64
Q_BLOCK = 128
EPS = 1e-6
ROPE_THETA = 10000.0
A_HEADS = 8
A_KV_HEADS = 2
A_GROUP = A_HEADS // A_KV_HEADS
A_HEAD_DIM = 64
A_WIDTH = A_HEADS * A_HEAD_DIM
A_KV_WIDTH = A_KV_HEADS * A_HEAD_DIM
B_HEADS = 4
B_HEAD_DIM = 64
B_V_DIM = 2 * B_HEAD_DIM
B_QK_WIDTH = B_HEADS * 2 * B_HEAD_DIM
B_WIDTH = B_HEADS * B_V_DIM
SPLIT_SIZES = (A_WIDTH, A_KV_WIDTH, A_KV_WIDTH, A_WIDTH,
               B_QK_WIDTH, B_QK_WIDTH, B_WIDTH, B_WIDTH,
               D_MODEL, D_MODEL)
D_IN = sum(SPLIT_SIZES)

kernel_name = "hybrid_gqa_diffattn_gated_encoder"


def rmsnorm(x, g):
    xf = x.astype(jnp.float32)
    y = xf * lax.rsqrt(jnp.mean(xf * xf, axis=-1, keepdims=True) + EPS)
    return (y * g.astype(jnp.float32)).astype(x.dtype)


def axial_rope_tables(seq_len):
    rows = seq_len // GRID_W
    row = jnp.repeat(jnp.arange(rows), GRID_W).astype(jnp.float32)
    col = jnp.tile(jnp.arange(GRID_W), rows).astype(jnp.float32)
    axis_dim = A_HEAD_DIM // 2
    inv_freq = ROPE_THETA ** (-jnp.arange(0, axis_dim, 2, dtype=jnp.float32) / axis_dim)
    ang_r = row[:, None, None] * inv_freq
    ang_c = col[:, None, None] * inv_freq
    return jnp.cos(ang_r), jnp.sin(ang_r), jnp.cos(ang_c), jnp.sin(ang_c)


def _rotate_half(x, cos, sin):
    x1, x2 = jnp.split(x, 2, axis=-1)
    return jnp.concatenate([x1 * cos - x2 * sin, x2 * cos + x1 * sin], axis=-1)


def axial_rope(x, tables):
    cos_r, sin_r, cos_c, sin_c = tables
    xr, xc = jnp.split(x.astype(jnp.float32), 2, axis=-1)
    out = jnp.concatenate([_rotate_half(xr, cos_r, sin_r), _rotate_half(xc, cos_c, sin_c)], axis=-1)
    return out.astype(x.dtype)


def gqa_attention(q, k, v):
    bn, s = q.shape[0], q.shape[1]
    nblk = s // Q_BLOCK
    qb = q.reshape(bn, nblk, Q_BLOCK, A_KV_HEADS, A_GROUP, A_HEAD_DIM).transpose(1, 0, 2, 3, 4, 5)
    kf = k.astype(jnp.float32)
    scale = A_HEAD_DIM ** -0.5

    def block(qi):
        sc = jnp.einsum('bqkgd,bskd->bkgqs', qi.astype(jnp.float32), kf) * scale
        p = jax.nn.softmax(sc, axis=-1).astype(v.dtype)
        return jnp.einsum('bkgqs,bskd->bqkgd', p, v)

    o = lax.map(block, qb)
    return o.transpose(1, 0, 2, 3, 4, 5).reshape(bn, s, A_WIDTH)


def alibi_slopes(n_heads):
    h = jnp.arange(1, n_heads + 1, dtype=jnp.float32)
    return 2.0 ** (-8.0 * h / n_heads)


def diff_attention(q1, q2, k1, k2, v, lam):
    bn, s = q1.shape[0], q1.shape[1]
    nblk = s // Q_BLOCK
    def to_blocks(a):
        return a.reshape(bn, nblk, Q_BLOCK, B_HEADS, B_HEAD_DIM).transpose(1, 0, 2, 3, 4)
    starts = jnp.arange(nblk) * Q_BLOCK
    k1f = k1.astype(jnp.float32)
    k2f = k2.astype(jnp.float32)
    slopes = alibi_slopes(B_HEADS)
    kpos = jnp.arange(s)
    scale = B_HEAD_DIM ** -0.5

    def block(args):
        q1i, q2i, start = args
        qpos = start + jnp.arange(Q_BLOCK)
        dist = jnp.abs(qpos[:, None] - kpos[None, :]).astype(jnp.float32)
        bias = -slopes[:, None, None] * dist
        s1 = jnp.einsum('bqhd,bshd->bhqs', q1i.astype(jnp.float32), k1f) * scale + bias
        s2 = jnp.einsum('bqhd,bshd->bhqs', q2i.astype(jnp.float32), k2f) * scale + bias
        p = jax.nn.softmax(s1, axis=-1) - lam * jax.nn.softmax(s2, axis=-1)
        return jnp.einsum('bhqs,bshe->bqhe', p.astype(v.dtype), v)

    o = lax.map(block, (to_blocks(q1), to_blocks(q2), starts))
    return o.transpose(1, 0, 2, 3, 4).reshape(bn, s, B_HEADS, B_V_DIM)


def encoder_trunk(x, c, w_ada, b_ada, norm_g, w_in, a_q_norm, a_k_norm,
                  b_lq1, b_lk1, b_lq2, b_lk2, b_sub_norm, p_a, p_b, w_out, final_g):
    bn, s, _ = x.shape
    rope = axial_rope_tables(s)
    c_act = jax.nn.silu(c)
    split_points = np.cumsum(SPLIT_SIZES)[:-1].tolist()
    for l in range(DEPTH):
        ada = c_act @ w_ada[l] + b_ada[l]
        shift, scl, gate = jnp.split(ada[:, None, :], 3, axis=-1)
        h = rmsnorm(x, norm_g[l]) * (1.0 + scl) + shift
        proj = h @ w_in[l]
        qa, ka, va, ga, qb, kb, vb, gb, ma, mb = jnp.split(proj, split_points, axis=-1)

        qa = axial_rope(rmsnorm(qa.reshape(bn, s, A_HEADS, A_HEAD_DIM), a_q_norm[l]), rope)
        ka = axial_rope(rmsnorm(ka.reshape(bn, s, A_KV_HEADS, A_HEAD_DIM), a_k_norm[l]), rope)
        va = va.reshape(bn, s, A_KV_HEADS, A_HEAD_DIM)
        oa = gqa_attention(qa.reshape(bn, s, A_KV_HEADS, A_GROUP, A_HEAD_DIM), ka, va)
        ya = (oa * jax.nn.silu(ga)) @ p_a[l]

        lam_init = 0.8 - 0.6 * math.exp(-0.3 * l)
        lam = (jnp.exp(jnp.sum(b_lq1[l].astype(jnp.float32) * b_lk1[l].astype(jnp.float32)))
               - jnp.exp(jnp.sum(b_lq2[l].astype(jnp.float32) * b_lk2[l].astype(jnp.float32)))
               + lam_init)
        qb = qb.reshape(bn, s, B_HEADS, 2, B_HEAD_DIM)
        kb = kb.reshape(bn, s, B_HEADS, 2, B_HEAD_DIM)
        vb = vb.reshape(bn, s, B_HEADS, B_V_DIM)
        ob = diff_attention(qb[..., 0, :], qb[..., 1, :], kb[..., 0, :], kb[..., 1, :], vb, lam)
        ob = (rmsnorm(ob, b_sub_norm[l]) * (1.0 - lam_init)).reshape(bn, s, B_WIDTH)
        yb = (ob * jax.nn.silu(gb)) @ p_b[l]

        merged = jax.nn.sigmoid(ma) * ya + jax.nn.sigmoid(mb) * yb
        x = x + gate * (merged @ w_out[l])
    return rmsnorm(x, final_g)


def setup_inputs(seed: int = 0) -> dict:
    key = jax.random.key(seed)
    ks = jax.random.split(key, 24)
    f32 = jnp.float32
    def nrm(k, shape, s):
        return jax.random.normal(k, shape, f32) * s
    def gain(k, shape):
        return 1.0 + 0.01 * jax.random.normal(k, shape, f32)
    return {
        "x_prompt": nrm(ks[0], (BATCH, SEQ, D_MODEL), 1.0),
        "x_sample": nrm(ks[1], (DEC_BATCH, DEC_SEQ, D_MODEL), 1.0),
        "c_prompt": nrm(ks[2], (BATCH, D_MODEL), 1.0),
        "c_sample": nrm(ks[3], (DEC_BATCH, D_MODEL), 1.0),
        "w_ada": nrm(ks[4], (DEPTH, D_MODEL, 3 * D_MODEL), D_MODEL ** -0.5),
        "b_ada": nrm(ks[5], (DEPTH, 3 * D_MODEL), 0.01),
        "norm_g": gain(ks[6], (DEPTH, D_MODEL)),
        "w_in": nrm(ks[7], (DEPTH, D_MODEL, D_IN), D_MODEL ** -0.5),
        "a_q_norm": gain(ks[8], (DEPTH, A_HEAD_DIM)),
        "a_k_norm": gain(ks[9], (DEPTH, A_HEAD_DIM)),
        "b_lq1": nrm(ks[10], (DEPTH, B_HEAD_DIM), 0.1),
        "b_lk1": nrm(ks[11], (DEPTH, B_HEAD_DIM), 0.1),
        "b_lq2": nrm(ks[12], (DEPTH, B_HEAD_DIM), 0.1),
        "b_lk2": nrm(ks[13], (DEPTH, B_HEAD_DIM), 0.1),
        "b_sub_norm": gain(ks[14], (DEPTH, B_V_DIM)),
        "p_a": nrm(ks[15], (DEPTH, A_WIDTH, D_MODEL), A_WIDTH ** -0.5),
        "p_b": nrm(ks[16], (DEPTH, B_WIDTH, D_MODEL), B_WIDTH ** -0.5),
        "w_out": nrm(ks[17], (DEPTH, D_MODEL, D_MODEL), D_MODEL ** -0.5),
        "final_g": gain(ks[18], (D_MODEL,)),
    }


def reference(x_prompt, x_sample, c_prompt, c_sample, w_ada, b_ada, norm_g, w_in,
              a_q_norm, a_k_norm, b_lq1, b_lk1, b_lq2, b_lk2, b_sub_norm,
              p_a, p_b, w_out, final_g):
    y_prompt = encoder_trunk(x_prompt, c_prompt, w_ada, b_ada, norm_g, w_in, a_q_norm, a_k_norm,
                             b_lq1, b_lk1, b_lq2, b_lk2, b_sub_norm, p_a, p_b, w_out, final_g)
    y_sample = encoder_trunk(x_sample, c_sample, w_ada, b_ada, norm_g, w_in, a_q_norm, a_k_norm,
                             b_lq1, b_lk1, b_lq2, b_lk2, b_sub_norm, p_a, p_b, w_out, final_g)
    return (y_prompt, y_sample)
```

```python
import functools
import math

import jax
import jax.numpy as jnp
from jax import lax
from jax.experimental import pallas as pl
from jax.experimental.pallas import tpu as pltpu

F32 = jnp.float32
BF16 = jnp.bfloat16

D_MODEL = 1024
GRID_W = 64
EPS = 1e-6
ROPE_THETA = 10000.0
A_HEADS = 8
A_KV_HEADS = 2
A_GROUP = A_HEADS // A_KV_HEADS
HEAD_DIM = 64
A_WIDTH = A_HEADS * HEAD_DIM
A_KV_WIDTH = A_KV_HEADS * HEAD_DIM
B_HEADS = 4
B_V_DIM = 2 * HEAD_DIM
B_WIDTH = B_HEADS * B_V_DIM
LAM_INIT = 0.8 - 0.6 * math.exp(-0.3 * 0)
ALIBI_SLOPES = tuple(2.0 ** (-8.0 * h / B_HEADS) for h in range(1, B_HEADS + 1))
QK_SCALE = HEAD_DIM ** -0.5

LANES = 128
VMEM_LIMIT = 56 * 1024 * 1024

_COLS = (("qa", A_WIDTH), ("ka", A_KV_WIDTH), ("va", A_KV_WIDTH), ("ga", A_WIDTH),
         ("qb", B_WIDTH), ("kb", B_WIDTH), ("vb", B_WIDTH), ("gb", B_WIDTH),
         ("ma", D_MODEL), ("mb", D_MODEL))
_OFF = {}
_o = 0
for _n, _w in _COLS:
    _OFF[_n] = (_o, _o + _w)
    _o += _w
D_IN = _o

TM = 512
TQ = 256


def _dot(a, b):
    return jnp.dot(a, b, preferred_element_type=F32)


def _dot_nt(a, b):
    return lax.dot_general(a, b, (((1,), (1,)), ((), ())), preferred_element_type=F32)


def _split_bf16(x):
    hi = x.astype(BF16)
    lo = (x - hi.astype(F32)).astype(BF16)
    return hi, lo


def _ada_kernel(c_ref, w_ref, b_ref, o_ref):
    c = c_ref[...]
    a = c * jax.nn.sigmoid(c)
    a_hi, a_lo = _split_bf16(a)
    w_hi, w_lo = _split_bf16(w_ref[...])
    acc = _dot(a_hi, w_hi) + (_dot(a_hi, w_lo) + _dot(a_lo, w_hi))
    o_ref[...] = acc + b_ref[...]


def _ada(c, w_ada, b_ada):
    bn = c.shape[0]
    n = w_ada.shape[1]
    tn = 512
    return pl.pallas_call(
        _ada_kernel,
        grid=(n // tn,),
        in_specs=[pl.BlockSpec((bn, D_MODEL), lambda j: (0, 0)),
                  pl.BlockSpec((D_MODEL, tn), lambda j: (0, j)),
                  pl.BlockSpec((1, tn), lambda j: (0, j))],
        out_specs=pl.BlockSpec((bn, tn), lambda j: (0, j)),
        out_shape=jax.ShapeDtypeStruct((bn, n), F32),
        name="ada",
    )(c, w_ada, b_ada.reshape(1, n))


def _swap16(t, low16):
    return jnp.where(low16, pltpu.roll(t, LANES - 16, axis=1), pltpu.roll(t, 16, axis=1))


def _head_norm_rope(y, gain, cos, sin, lo64, low16):
    sq = y * y
    s_all = jnp.sum(sq, axis=-1, keepdims=True)
    s_lo = jnp.sum(jnp.where(lo64, sq, 0.0), axis=-1, keepdims=True)
    r_lo = lax.rsqrt(s_lo * (1.0 / HEAD_DIM) + EPS)
    r_hi = lax.rsqrt((s_all - s_lo) * (1.0 / HEAD_DIM) + EPS)
    t = y * gain
    u = t * cos + _swap16(t, low16) * sin
    return u * jnp.where(lo64, r_lo, r_hi)


def _in_proj_kernel(x_ref, ada_ref, ng_ref, w_ref, gq_ref, gk_ref, cos_ref, sin_ref,
                    qa_ref, ka_ref, va_ref, ga_ref, qb_ref, kb_ref, vb_ref, gb_ref,
                    ma_ref, mb_ref):
    x = x_ref[0]
    ada = ada_ref[0]
    shift, scl = ada[0:1, :], ada[1:2, :]
    ms = jnp.mean(x * x, axis=-1, keepdims=True)
    h = (x * lax.rsqrt(ms + EPS)) * (ng_ref[...] * (1.0 + scl)) + shift
    hb = h.astype(BF16)

    lane = lax.broadcasted_iota(jnp.int32, (1, LANES), 1)
    lo64 = lane < HEAD_DIM
    low16 = (lane % 32) < 16
    cos = cos_ref[...]
    sin = sin_ref[...]

    def proj(name):
        c0, c1 = _OFF[name]
        return _dot(hb, w_ref[:, c0:c1])

    y = proj("qa")
    gq = gq_ref[...]
    for j in range(A_WIDTH // LANES):
        sl = slice(j * LANES, (j + 1) * LANES)
        u = _head_norm_rope(y[:, sl], gq, cos, sin, lo64, low16)
        qa_ref[0, :, sl] = (u * QK_SCALE).astype(BF16)
    y = proj("ka")
    ka_ref[0] = _head_norm_rope(y, gk_ref[...], cos, sin, lo64, low16).astype(BF16)
    va_ref[0] = proj("va").astype(BF16)
    y = proj("ga")
    ga_ref[0] = (y * jax.nn.sigmoid(y)).astype(BF16)
    qb_ref[0] = (proj("qb") * QK_SCALE).astype(BF16)
    kb_ref[0] = proj("kb").astype(BF16)
    vb_ref[0] = proj("vb").astype(BF16)
    y = proj("gb")
    gb_ref[0] = (y * jax.nn.sigmoid(y)).astype(BF16)
    ma_ref[0] = jax.nn.sigmoid(proj("ma")).astype(BF16)
    mb_ref[0] = jax.nn.sigmoid(proj("mb")).astype(BF16)


def _in_proj(x, ada3, norm_g, w_all, gq, gk, cos, sin):
    bn, s, _ = x.shape
    const = lambda b, i: (0, 0)
    row = lambda b, i: (b, i, 0)
    widths = [w for _, w in _COLS]
    return pl.pallas_call(
        _in_proj_kernel,
        grid=(bn, s // TM),
        in_specs=[pl.BlockSpec((1, TM, D_MODEL), row),
                  pl.BlockSpec((1, 3, D_MODEL), lambda b, i: (b, 0, 0)),
                  pl.BlockSpec((1, D_MODEL), const),
                  pl.BlockSpec((D_MODEL, D_IN), const, pipeline_mode=pl.Buffered(1)),
                  pl.BlockSpec((1, LANES), const),
                  pl.BlockSpec((1, LANES), const),
                  pl.BlockSpec((TM, LANES), lambda b, i: (i, 0)),
                  pl.BlockSpec((TM, LANES), lambda b, i: (i, 0))],
        out_specs=[pl.BlockSpec((1, TM, w), row) for w in widths],
        out_shape=[jax.ShapeDtypeStruct((bn, s, w), BF16) for w in widths],
        compiler_params=pltpu.CompilerParams(
            dimension_semantics=("parallel", "arbitrary"), vmem_limit_bytes=VMEM_LIMIT),
        name="in_proj",
    )(x, ada3, norm_g, w_all, gq, gk, cos, sin)


def _softmax_rows(s):
    m = jnp.max(s, axis=-1, keepdims=True)
    p = jnp.exp(s - m)
    return p, jnp.sum(p, axis=-1, keepdims=True)


def _attn_a_kernel(q_ref, k_ref, v_ref, g_ref, o_ref):
    k = k_ref[0]
    v = v_ref[0]
    lane = lax.broadcasted_iota(jnp.int32, (1, LANES), 1)
    lo64 = lane < HEAD_DIM
    zero = jnp.zeros((), BF16)
    for j in range(A_WIDTH // LANES):
        sl = slice(j * LANES, (j + 1) * LANES)
        q = q_ref[0, :, sl]
        outs = []
        for keep in (lo64, jnp.logical_not(lo64)):
            s = _dot_nt(jnp.where(keep, q, zero), k)
            p, l = _softmax_rows(s)
            outs.append(_dot(p.astype(BF16), v) * (1.0 / l))
        o = jnp.where(lo64, outs[0], outs[1])
        o_ref[0, :, sl] = (o * g_ref[0, :, sl].astype(F32)).astype(BF16)


def _attn_a(qa, ka, va, ga):
    bn, s, _ = qa.shape
    qrow = lambda b, i: (b, i, 0)
    full = lambda b, i: (b, 0, 0)
    return pl.pallas_call(
        _attn_a_kernel,
        grid=(bn, s // TQ),
        in_specs=[pl.BlockSpec((1, TQ, A_WIDTH), qrow),
                  pl.BlockSpec((1, s, A_KV_WIDTH), full),
                  pl.BlockSpec((1, s, A_KV_WIDTH), full),
                  pl.BlockSpec((1, TQ, A_WIDTH), qrow)],
        out_specs=pl.BlockSpec((1, TQ, A_WIDTH), qrow),
        out_shape=jax.ShapeDtypeStruct((bn, s, A_WIDTH), BF16),
        compiler_params=pltpu.CompilerParams(
            dimension_semantics=("parallel", "arbitrary"), vmem_limit_bytes=VMEM_LIMIT),
        name="attn_a",
    )(qa, ka, va, ga)


def _attn_b_kernel(q_ref, k_ref, v_ref, g_ref, lqk_ref, gsub_ref, o_ref):
    tq = q_ref.shape[1]
    s_len = k_ref.shape[1]
    lqk = lqk_ref[...]
    lam = (jnp.exp(jnp.sum(lqk[0:1] * lqk[1:2], axis=-1, keepdims=True))
           - jnp.exp(jnp.sum(lqk[2:3] * lqk[3:4], axis=-1, keepdims=True)) + LAM_INIT)
    lane = lax.broadcasted_iota(jnp.int32, (1, LANES), 1)
    lo64 = lane < HEAD_DIM
    zero = jnp.zeros((), BF16)
    qpos = pl.program_id(1) * tq + lax.broadcasted_iota(jnp.int32, (tq, 1), 0)
    kpos = lax.broadcasted_iota(jnp.int32, (1, s_len), 1)
    dist = jnp.abs(qpos - kpos).astype(F32)
    gsub = gsub_ref[...] * (1.0 - LAM_INIT)
    for h in range(B_HEADS):
        sl = slice(h * LANES, (h + 1) * LANES)
        q = q_ref[0, :, sl]
        k = k_ref[0, :, sl]
        bias = dist * (-ALIBI_SLOPES[h])
        p1, l1 = _softmax_rows(_dot_nt(jnp.where(lo64, q, zero), k) + bias)
        p2, l2 = _softmax_rows(_dot_nt(jnp.where(lo64, zero, q), k) + bias)
        p = p1 * (1.0 / l1) - p2 * (lam / l2)
        o = _dot(p.astype(BF16), v_ref[0, :, sl])
        o = o * lax.rsqrt(jnp.mean(o * o, axis=-1, keepdims=True) + EPS) * gsub
        o_ref[0, :, sl] = (o * g_ref[0, :, sl].astype(F32)).astype(BF16)


def _attn_b(qb, kb, vb, gb, lqk, gsub):
    bn, s, _ = qb.shape
    qrow = lambda b, i: (b, i, 0)
    full = lambda b, i: (b, 0, 0)
    const = lambda b, i: (0, 0)
    return pl.pallas_call(
        _attn_b_kernel,
        grid=(bn, s // TQ),
        in_specs=[pl.BlockSpec((1, TQ, B_WIDTH), qrow),
                  pl.BlockSpec((1, s, B_WIDTH), full),
                  pl.BlockSpec((1, s, B_WIDTH), full),
                  pl.BlockSpec((1, TQ, B_WIDTH), qrow),
                  pl.BlockSpec((4, HEAD_DIM), const),
                  pl.BlockSpec((1, B_V_DIM), const)],
        out_specs=pl.BlockSpec((1, TQ, B_WIDTH), qrow),
        out_shape=jax.ShapeDtypeStruct((bn, s, B_WIDTH), BF16),
        compiler_params=pltpu.CompilerParams(
            dimension_semantics=("parallel", "arbitrary"), vmem_limit_bytes=VMEM_LIMIT),
        name="attn_b",
    )(qb, kb, vb, gb, lqk, gsub)


def _out_proj_kernel(x_ref, ada_ref, za_ref, zb_ref, ma_ref, mb_ref,
                     pa_ref, pb_ref, wo_ref, fg_ref, y_ref):
    gate = ada_ref[0][2:3, :]
    ya = _dot(za_ref[0], pa_ref[...])
    yb = _dot(zb_ref[0], pb_ref[...])
    merged = ma_ref[0].astype(F32) * ya + mb_ref[0].astype(F32) * yb
    out = x_ref[0] + gate * _dot(merged.astype(BF16), wo_ref[...])
    r = lax.rsqrt(jnp.mean(out * out, axis=-1, keepdims=True) + EPS)
    y_ref[0] = out * r * fg_ref[...]


def _out_proj(x, ada3, za, zb, ma, mb, pa, pb, wo, fg):
    bn, s, _ = x.shape
    const = lambda b, i: (0, 0)
    row = lambda b, i: (b, i, 0)
    return pl.pallas_call(
        _out_proj_kernel,
        grid=(bn, s // TM),
        in_specs=[pl.BlockSpec((1, TM, D_MODEL), row),
                  pl.BlockSpec((1, 3, D_MODEL), lambda b, i: (b, 0, 0)),
                  pl.BlockSpec((1, TM, A_WIDTH), row),
                  pl.BlockSpec((1, TM, B_WIDTH), row),
                  pl.BlockSpec((1, TM, D_MODEL), row),
                  pl.BlockSpec((1, TM, D_MODEL), row),
                  pl.BlockSpec((A_WIDTH, D_MODEL), const),
                  pl.BlockSpec((B_WIDTH, D_MODEL), const),
                  pl.BlockSpec((D_MODEL, D_MODEL), const),
                  pl.BlockSpec((1, D_MODEL), const)],
        out_specs=pl.BlockSpec((1, TM, D_MODEL), row),
        out_shape=jax.ShapeDtypeStruct((bn, s, D_MODEL), F32),
        compiler_params=pltpu.CompilerParams(
            dimension_semantics=("parallel", "arbitrary"), vmem_limit_bytes=VMEM_LIMIT),
        name="out_proj",
    )(x, ada3, za, zb, ma, mb, pa, pb, wo, fg)


def _rope_tables(s):
    rows = s // GRID_W
    row = jnp.repeat(jnp.arange(rows), GRID_W).astype(F32)
    col = jnp.tile(jnp.arange(GRID_W), rows).astype(F32)
    axis_dim = HEAD_DIM // 2
    inv_freq = ROPE_THETA ** (-jnp.arange(0, axis_dim, 2, dtype=F32) / axis_dim)
    ang_r = row[:, None] * inv_freq
    ang_c = col[:, None] * inv_freq
    cos = jnp.concatenate([jnp.cos(ang_r)] * 2 + [jnp.cos(ang_c)] * 2, axis=-1)
    sin = jnp.concatenate([-jnp.sin(ang_r), jnp.sin(ang_r),
                           -jnp.sin(ang_c), jnp.sin(ang_c)], axis=-1)
    return jnp.tile(cos, (1, 2)), jnp.tile(sin, (1, 2))


def _pair_perm():
    idx = []
    for j in range(A_GROUP):
        for head in (j, A_GROUP + j):
            idx.extend(range(head * HEAD_DIM, (head + 1) * HEAD_DIM))
    return jnp.asarray(idx, jnp.int32)


def _prep_weights(w_in, p_a, p_b, w_out):
    bounds = [0]
    for wdt in (A_WIDTH, A_KV_WIDTH, A_KV_WIDTH, A_WIDTH, B_WIDTH, B_WIDTH, B_WIDTH, B_WIDTH,
                D_MODEL, D_MODEL):
        bounds.append(bounds[-1] + wdt)
    parts = [w_in[:, bounds[i]:bounds[i + 1]] for i in range(10)]
    perm = _pair_perm()
    parts[0] = parts[0][:, perm]
    parts[3] = parts[3][:, perm]
    w_all = jnp.concatenate(parts, axis=1).astype(BF16)
    return w_all, p_a[perm, :].astype(BF16), p_b.astype(BF16), w_out.astype(BF16)


def _trunk(x, c, w_ada, b_ada, norm_g, w_all, gq, gk, lqk, gsub, pa, pb, wo, fg):
    bn = x.shape[0]
    cos, sin = _rope_tables(x.shape[1])
    ada3 = _ada(c, w_ada, b_ada).reshape(bn, 3, D_MODEL)
    qa, ka, va, ga, qb, kb, vb, gb, ma, mb = _in_proj(x, ada3, norm_g, w_all, gq, gk, cos, sin)
    za = _attn_a(qa, ka, va, ga)
    zb = _attn_b(qb, kb, vb, gb, lqk, gsub)
    return _out_proj(x, ada3, za, zb, ma, mb, pa, pb, wo, fg)


def kernel(x_prompt, x_sample, c_prompt, c_sample, w_ada, b_ada, norm_g, w_in, a_q_norm, a_k_norm,
           b_lq1, b_lk1, b_lq2, b_lk2, b_sub_norm, p_a, p_b, w_out, final_g):
    w_all, pa, pb, wo = _prep_weights(w_in[0], p_a[0], p_b[0], w_out[0])
    gq = jnp.tile(a_q_norm[0], 2).reshape(1, LANES)
    gk = jnp.tile(a_k_norm[0], 2).reshape(1, LANES)
    lqk = jnp.stack([b_lq1[0], b_lk1[0], b_lq2[0], b_lk2[0]]).astype(F32)
    gsub = b_sub_norm[0].reshape(1, B_V_DIM)
    fg = final_g.reshape(1, D_MODEL)
    args = (w_ada[0], b_ada[0], norm_g, w_all, gq, gk, lqk, gsub, pa, pb, wo, fg)
    y_prompt = _trunk(x_prompt, c_prompt, *args)
    y_sample = _trunk(x_sample, c_sample, *args)
    return (y_prompt, y_sample)
```

```python
import functools
import math

import jax
import jax.numpy as jnp
from jax import lax
from jax.experimental import pallas as pl
from jax.experimental.pallas import tpu as pltpu

F32 = jnp.float32
BF16 = jnp.bfloat16

D_MODEL = 1024
GRID_W = 64
EPS = 1e-6
ROPE_THETA = 10000.0
A_HEADS = 8
A_KV_HEADS = 2
A_GROUP = A_HEADS // A_KV_HEADS
HEAD_DIM = 64
A_WIDTH = A_HEADS * HEAD_DIM
A_KV_WIDTH = A_KV_HEADS * HEAD_DIM
B_HEADS = 4
B_V_DIM = 2 * HEAD_DIM
B_WIDTH = B_HEADS * B_V_DIM
LAM_INIT = 0.8 - 0.6 * math.exp(-0.3 * 0)
ALIBI_SLOPES = tuple(2.0 ** (-8.0 * h / B_HEADS) for h in range(1, B_HEADS + 1))
QK_SCALE = HEAD_DIM ** -0.5

LANES = 128
SUBLANES = 8
KEY_CHUNK = 256
ONES_ROWS = 16
CHUNK_UNROLL = 8
LOG2E = math.log2(math.e)
VMEM_LIMIT = 56 * 1024 * 1024

_COLS = (("qa", A_WIDTH), ("ka", A_KV_WIDTH), ("va", A_KV_WIDTH), ("ga", A_WIDTH),
         ("qb", B_WIDTH), ("kb", B_WIDTH), ("vb", B_WIDTH), ("gb", B_WIDTH),
         ("ma", D_MODEL), ("mb", D_MODEL))
_OFF = {}
_o = 0
for _n, _w in _COLS:
    _OFF[_n] = (_o, _o + _w)
    _o += _w
D_IN = _o
A_PAIRS = A_WIDTH // LANES
N_SLABS = 4
assert A_PAIRS == N_SLABS and B_HEADS == N_SLABS and B_V_DIM == LANES
SLAB_MAJOR = ("qa", "ga", "qb", "kb", "vb", "gb")

TM = 512
TQ_A = 512
TQ_B = 512


def _dot(a, b):
    return jnp.dot(a, b, preferred_element_type=F32)


def _dot_nt(a, b):
    return lax.dot_general(a, b, (((1,), (1,)), ((), ())), preferred_element_type=F32)


def _split_bf16(x):
    hi = x.astype(BF16)
    lo = (x - hi.astype(F32)).astype(BF16)
    return hi, lo


def _ada_kernel(c_ref, w_ref, b_ref, o_ref):
    c = c_ref[...]
    a = c * jax.nn.sigmoid(c)
    a_hi, a_lo = _split_bf16(a)
    w_hi, w_lo = _split_bf16(w_ref[...])
    acc = _dot(a_hi, w_hi) + (_dot(a_hi, w_lo) + _dot(a_lo, w_hi))
    o_ref[...] = acc + b_ref[...]


def _ada(c, w_ada, b_ada):
    bn = c.shape[0]
    n = w_ada.shape[1]
    tn = 512
    return pl.pallas_call(
        _ada_kernel,
        grid=(n // tn,),
        in_specs=[pl.BlockSpec((bn, D_MODEL), lambda j: (0, 0)),
                  pl.BlockSpec((D_MODEL, tn), lambda j: (0, j)),
                  pl.BlockSpec((1, tn), lambda j: (0, j))],
        out_specs=pl.BlockSpec((bn, tn), lambda j: (0, j)),
        out_shape=jax.ShapeDtypeStruct((bn, n), F32),
        name="ada",
    )(c, w_ada, b_ada.reshape(1, n))


def _swap16(t, low16):
    return jnp.where(low16, pltpu.roll(t, LANES - 16, axis=1), pltpu.roll(t, 16, axis=1))


def _head_norm_rope(y, gain, cos, sin, lo64, low16):
    sq = y * y
    s_all = jnp.sum(sq, axis=-1, keepdims=True)
    s_lo = jnp.sum(jnp.where(lo64, sq, 0.0), axis=-1, keepdims=True)
    r_lo = lax.rsqrt(s_lo * (1.0 / HEAD_DIM) + EPS)
    r_hi = lax.rsqrt((s_all - s_lo) * (1.0 / HEAD_DIM) + EPS)
    t = y * gain
    u = t * cos + _swap16(t, low16) * sin
    return u * jnp.where(lo64, r_lo, r_hi)


def _in_proj_kernel(x_ref, ada_ref, ng_ref, w_ref, gq_ref, gk_ref, cos_ref, sin_ref,
                    qa_ref, ka_ref, va_ref, ga_ref, qb_ref, kb_ref, vb_ref, gb_ref,
                    ma_ref, mb_ref):
    x = x_ref[0]
    ada = ada_ref[0]
    shift, scl = ada[0:1, :], ada[1:2, :]
    ms = jnp.mean(x * x, axis=-1, keepdims=True)
    h = (x * lax.rsqrt(ms + EPS)) * (ng_ref[...] * (1.0 + scl)) + shift
    hb = h.astype(BF16)

    lane = lax.broadcasted_iota(jnp.int32, (1, LANES), 1)
    lo64 = lane < HEAD_DIM
    low16 = (lane % 32) < 16
    cos = cos_ref[...]
    sin = sin_ref[...]

    def proj(name):
        c0, c1 = _OFF[name]
        return _dot(hb, w_ref[:, c0:c1])

    def store_slabs(ref, y):
        y = y.astype(BF16)
        for j in range(N_SLABS):
            ref[0, j] = y[:, j * LANES:(j + 1) * LANES]

    y = proj("qa")
    gq = gq_ref[...]
    for j in range(N_SLABS):
        sl = slice(j * LANES, (j + 1) * LANES)
        u = _head_norm_rope(y[:, sl], gq, cos, sin, lo64, low16)
        qa_ref[0, j] = (u * (QK_SCALE * LOG2E)).astype(BF16)
    y = proj("ka")
    ka_ref[0] = _head_norm_rope(y, gk_ref[...], cos, sin, lo64, low16).astype(BF16)
    va_ref[0] = proj("va").astype(BF16)
    y = proj("ga")
    store_slabs(ga_ref, y * jax.nn.sigmoid(y))
    store_slabs(qb_ref, proj("qb") * (QK_SCALE * LOG2E))
    store_slabs(kb_ref, proj("kb"))
    store_slabs(vb_ref, proj("vb"))
    y = proj("gb")
    store_slabs(gb_ref, y * jax.nn.sigmoid(y))
    ma_ref[0] = jax.nn.sigmoid(proj("ma")).astype(BF16)
    mb_ref[0] = jax.nn.sigmoid(proj("mb")).astype(BF16)


def _in_proj(x, ada3, norm_g, w_all, gq, gk, cos, sin):
    bn, s, _ = x.shape
    const = lambda b, i: (0, 0)
    row = lambda b, i: (b, i, 0)
    slab_spec = pl.BlockSpec((1, N_SLABS, TM, LANES), lambda b, i: (b, 0, i, 0))
    return pl.pallas_call(
        _in_proj_kernel,
        grid=(bn, s // TM),
        in_specs=[pl.BlockSpec((1, TM, D_MODEL), row),
                  pl.BlockSpec((1, 3, D_MODEL), lambda b, i: (b, 0, 0)),
                  pl.BlockSpec((1, D_MODEL), const),
                  pl.BlockSpec((D_MODEL, D_IN), const, pipeline_mode=pl.Buffered(1)),
                  pl.BlockSpec((1, LANES), const),
                  pl.BlockSpec((1, LANES), const),
                  pl.BlockSpec((TM, LANES), lambda b, i: (i, 0)),
                  pl.BlockSpec((TM, LANES), lambda b, i: (i, 0))],
        out_specs=[slab_spec if n in SLAB_MAJOR else pl.BlockSpec((1, TM, w), row)
                   for n, w in _COLS],
        out_shape=[jax.ShapeDtypeStruct((bn, N_SLABS, s, LANES) if n in SLAB_MAJOR else (bn, s, w), BF16)
                   for n, w in _COLS],
        compiler_params=pltpu.CompilerParams(
            dimension_semantics=("parallel", "arbitrary"), vmem_limit_bytes=VMEM_LIMIT),
        name="in_proj",
    )(x, ada3, norm_g, w_all, gq, gk, cos, sin)


def _softmax_rows(s):
    m = jnp.max(s, axis=-1, keepdims=True)
    p = jnp.exp(s - m)
    return p, jnp.sum(p, axis=-1, keepdims=True)


def _col_max(s, m8):
    return jnp.maximum(m8, jnp.max(s.reshape(s.shape[0] // SUBLANES, SUBLANES, s.shape[1]), axis=0))


def _attn_a_kernel(q_ref, k_ref, v_ref, g_ref, o_ref, vt_ref, qm_ref, s0_ref, s1_ref, ot_ref):
    tq = q_ref.shape[2]
    s_len = k_ref.shape[1]
    n_chunks = s_len // KEY_CHUNK

    @pl.when(pl.program_id(1) == 0)
    def _():
        vt = v_ref[0].astype(F32).T
        for g in range(A_KV_HEADS):
            for c in range(n_chunks):
                cols = slice(c * KEY_CHUNK, (c + 1) * KEY_CHUNK)
                vt_ref[g, c, 0:HEAD_DIM, :] = vt[g * HEAD_DIM:(g + 1) * HEAD_DIM, cols].astype(BF16)
                vt_ref[g, c, HEAD_DIM:, :] = jnp.ones((ONES_ROWS, KEY_CHUNK), BF16)

    lane = lax.broadcasted_iota(jnp.int32, (1, LANES), 1)
    lo64 = lane < HEAD_DIM
    zero = jnp.zeros((), BF16)
    for n in range(A_HEADS):
        keep = lo64 if n % 2 == 0 else jnp.logical_not(lo64)
        qm_ref[n] = jnp.where(keep, q_ref[0, n // 2], zero)

    s_bufs = (s0_ref, s1_ref)
    m8_init = jnp.full((SUBLANES, tq), -jnp.inf, F32)

    def run_stage(kv, m8, n_next):
        m = None if m8 is None else jnp.max(m8, axis=0, keepdims=True)
        qm = None if n_next is None else qm_ref[n_next]
        acc = jnp.zeros((HEAD_DIM + ONES_ROWS, tq), F32)
        m8_next = m8_init
        for c in range(n_chunks):
            rows = slice(c * KEY_CHUNK, (c + 1) * KEY_CHUNK)
            if qm is not None:
                sc = _dot_nt(k_ref[0, rows, :], qm)
                s_bufs[1 - kv][rows, :] = sc
                m8_next = _col_max(sc, m8_next)
            if m is not None:
                p = jnp.exp2(s_bufs[kv][rows, :] - m).astype(BF16)
                acc = acc + _dot(vt_ref[kv, c], p)
        if m is not None:
            ot_ref[kv * HEAD_DIM:(kv + 1) * HEAD_DIM, :] = (
                acc[0:HEAD_DIM] * (1.0 / acc[HEAD_DIM:HEAD_DIM + 1]))
        return m8_next

    def emit_pair(j):
        o = ot_ref[...].T
        o_ref[0, j] = (o * g_ref[0, j].astype(F32)).astype(BF16)

    def even_stage(n, m8):
        return run_stage(0, m8, n + 1)

    def odd_stage(n, m8):
        m8_next = run_stage(1, m8, n + 1)
        emit_pair(lax.shift_right_logical(n, 1))
        return m8_next

    def stage(n, m8):
        return lax.cond((n & 1) == 0, functools.partial(even_stage, n),
                        functools.partial(odd_stage, n), m8)

    m8 = lax.fori_loop(0, A_HEADS - 1, stage, run_stage(1, None, 0))
    run_stage(1, m8, None)
    emit_pair(A_PAIRS - 1)


def _attn_a(qa, ka, va, ga):
    bn, _, s, _ = qa.shape
    qrow = lambda b, i: (b, 0, i, 0)
    full = lambda b, i: (b, 0, 0)
    return pl.pallas_call(
        _attn_a_kernel,
        grid=(bn, s // TQ_A),
        in_specs=[pl.BlockSpec((1, A_PAIRS, TQ_A, LANES), qrow),
                  pl.BlockSpec((1, s, A_KV_WIDTH), full),
                  pl.BlockSpec((1, s, A_KV_WIDTH), full),
                  pl.BlockSpec((1, A_PAIRS, TQ_A, LANES), qrow)],
        out_specs=pl.BlockSpec((1, A_PAIRS, TQ_A, LANES), qrow),
        out_shape=jax.ShapeDtypeStruct((bn, A_PAIRS, s, LANES), BF16),
        scratch_shapes=[pltpu.VMEM((A_KV_HEADS, s // KEY_CHUNK, HEAD_DIM + ONES_ROWS, KEY_CHUNK), BF16),
                        pltpu.VMEM((A_HEADS, TQ_A, LANES), BF16),
                        pltpu.VMEM((s, TQ_A), F32),
                        pltpu.VMEM((s, TQ_A), F32),
                        pltpu.VMEM((LANES, TQ_A), F32)],
        compiler_params=pltpu.CompilerParams(
            dimension_semantics=("parallel", "arbitrary"), vmem_limit_bytes=VMEM_LIMIT),
        name="attn_a",
    )(qa, ka, va, ga)


def _attn_b_kernel(slope_ref, q_ref, k_ref, v_ref, g_ref, lqk_ref, gsub_ref, o_ref,
                   vt_ref, qm_ref, s0_ref, s1_ref, dist_ref, o1_ref):
    tq = q_ref.shape[2]
    s_len = k_ref.shape[2]
    n_chunks = s_len // KEY_CHUNK
    n_units = 2 * B_HEADS

    @pl.when(pl.program_id(1) == 0)
    def _():
        for h in range(B_HEADS):
            vt = v_ref[0, h].astype(F32).T
            for c in range(n_chunks):
                cols = slice(c * KEY_CHUNK, (c + 1) * KEY_CHUNK)
                vt_ref[h, c, 0:B_V_DIM, :] = vt[:, cols].astype(BF16)
                vt_ref[h, c, B_V_DIM:, :] = jnp.ones((ONES_ROWS, KEY_CHUNK), BF16)

    lqk = lqk_ref[...]
    lam = (jnp.exp(jnp.sum(lqk[0:1] * lqk[1:2], axis=-1, keepdims=True))
           - jnp.exp(jnp.sum(lqk[2:3] * lqk[3:4], axis=-1, keepdims=True)) + LAM_INIT)
    lane = lax.broadcasted_iota(jnp.int32, (1, LANES), 1)
    lo64 = lane < HEAD_DIM
    zero = jnp.zeros((), BF16)
    for n in range(n_units):
        keep = lo64 if n % 2 == 0 else jnp.logical_not(lo64)
        qm_ref[n] = jnp.where(keep, q_ref[0, n // 2], zero)

    rel = (lax.broadcasted_iota(jnp.int32, (KEY_CHUNK, tq), 0)
           - lax.broadcasted_iota(jnp.int32, (KEY_CHUNK, tq), 1))
    t0 = pl.program_id(1) * tq
    for c in range(n_chunks):
        dist_ref[c * KEY_CHUNK:(c + 1) * KEY_CHUNK, :] = jnp.abs(rel + (c * KEY_CHUNK - t0)).astype(F32)

    s_bufs = (s0_ref, s1_ref)
    m8_init = jnp.full((SUBLANES, tq), -jnp.inf, F32)

    def run_stage(par, m8, n_fin, n_next):
        acc = jnp.zeros((B_V_DIM + ONES_ROWS, tq), F32)
        m8_next = m8_init
        if m8 is not None:
            m = jnp.max(m8, axis=0, keepdims=True)
            h_fin = lax.shift_right_logical(n_fin, 1)
        if n_next is not None:
            qm = qm_ref[n_next]
            h_next = lax.shift_right_logical(n_next, 1)
            slope = slope_ref[h_next]
        for c in range(n_chunks):
            rows = slice(c * KEY_CHUNK, (c + 1) * KEY_CHUNK)
            if n_next is not None:
                sc = _dot_nt(k_ref[0, h_next, rows, :], qm) + dist_ref[rows, :] * slope
                s_bufs[1 - par][rows, :] = sc
                m8_next = _col_max(sc, m8_next)
            if m8 is not None:
                p = jnp.exp2(s_bufs[par][rows, :] - m).astype(BF16)
                acc = acc + _dot(vt_ref[h_fin, c], p)
        if m8 is not None:
            o = acc[0:B_V_DIM] * (1.0 / acc[B_V_DIM:B_V_DIM + 1])
            if par == 0:
                o1_ref[...] = o
            else:
                o = o1_ref[...] - lam * o
                o = o * lax.rsqrt(jnp.mean(o * o, axis=0, keepdims=True) + EPS)
                o = o.T * (gsub_ref[...] * (1.0 - LAM_INIT))
                o_ref[0, h_fin] = (o * g_ref[0, h_fin].astype(F32)).astype(BF16)
        return m8_next

    def stage(n, m8):
        return lax.cond((n & 1) == 0,
                        lambda m: run_stage(0, m, n, n + 1),
                        lambda m: run_stage(1, m, n, n + 1), m8)

    m8 = lax.fori_loop(0, n_units - 1, stage, run_stage(1, None, None, jnp.int32(0)))
    run_stage(1, m8, jnp.int32(n_units - 1), None)


def _attn_b(qb, kb, vb, gb, lqk, gsub):
    bn, _, s, _ = qb.shape
    qrow = lambda b, i: (b, 0, i, 0)
    full = lambda b, i: (b, 0, 0, 0)
    const = lambda b, i: (0, 0)
    slopes = jnp.asarray([-sl * LOG2E for sl in ALIBI_SLOPES], F32)
    return pl.pallas_call(
        _attn_b_kernel,
        grid=(bn, s // TQ_B),
        in_specs=[pl.BlockSpec(memory_space=pltpu.SMEM),
                  pl.BlockSpec((1, B_HEADS, TQ_B, LANES), qrow),
                  pl.BlockSpec((1, B_HEADS, s, LANES), full),
                  pl.BlockSpec((1, B_HEADS, s, LANES), full),
                  pl.BlockSpec((1, B_HEADS, TQ_B, LANES), qrow),
                  pl.BlockSpec((4, HEAD_DIM), const),
                  pl.BlockSpec((1, B_V_DIM), const)],
        out_specs=pl.BlockSpec((1, B_HEADS, TQ_B, LANES), qrow),
        out_shape=jax.ShapeDtypeStruct((bn, B_HEADS, s, LANES), BF16),
        scratch_shapes=[pltpu.VMEM((B_HEADS, s // KEY_CHUNK, B_V_DIM + ONES_ROWS, KEY_CHUNK), BF16),
                        pltpu.VMEM((2 * B_HEADS, TQ_B, LANES), BF16),
                        pltpu.VMEM((s, TQ_B), F32),
                        pltpu.VMEM((s, TQ_B), F32),
                        pltpu.VMEM((s, TQ_B), F32),
                        pltpu.VMEM((B_V_DIM, TQ_B), F32)],
        compiler_params=pltpu.CompilerParams(
            dimension_semantics=("parallel", "arbitrary"), vmem_limit_bytes=VMEM_LIMIT),
        name="attn_b",
    )(slopes, qb, kb, vb, gb, lqk, gsub)


def _out_proj_kernel(x_ref, ada_ref, za_ref, zb_ref, ma_ref, mb_ref,
                     pa_ref, pb_ref, wo_ref, fg_ref, y_ref):
    gate = ada_ref[0][2:3, :]
    za = jnp.concatenate([za_ref[0, j] for j in range(N_SLABS)], axis=1)
    zb = jnp.concatenate([zb_ref[0, j] for j in range(N_SLABS)], axis=1)
    ya = _dot(za, pa_ref[...])
    yb = _dot(zb, pb_ref[...])
    merged = ma_ref[0].astype(F32) * ya + mb_ref[0].astype(F32) * yb
    out = x_ref[0] + gate * _dot(merged.astype(BF16), wo_ref[...])
    r = lax.rsqrt(jnp.mean(out * out, axis=-1, keepdims=True) + EPS)
    y_ref[0] = out * r * fg_ref[...]


def _out_proj(x, ada3, za, zb, ma, mb, pa, pb, wo, fg):
    bn, s, _ = x.shape
    const = lambda b, i: (0, 0)
    row = lambda b, i: (b, i, 0)
    return pl.pallas_call(
        _out_proj_kernel,
        grid=(bn, s // TM),
        in_specs=[pl.BlockSpec((1, TM, D_MODEL), row),
                  pl.BlockSpec((1, 3, D_MODEL), lambda b, i: (b, 0, 0)),
                  pl.BlockSpec((1, N_SLABS, TM, LANES), lambda b, i: (b, 0, i, 0)),
                  pl.BlockSpec((1, N_SLABS, TM, LANES), lambda b, i: (b, 0, i, 0)),
                  pl.BlockSpec((1, TM, D_MODEL), row),
                  pl.BlockSpec((1, TM, D_MODEL), row),
                  pl.BlockSpec((A_WIDTH, D_MODEL), const),
                  pl.BlockSpec((B_WIDTH, D_MODEL), const),
                  pl.BlockSpec((D_MODEL, D_MODEL), const),
                  pl.BlockSpec((1, D_MODEL), const)],
        out_specs=pl.BlockSpec((1, TM, D_MODEL), row),
        out_shape=jax.ShapeDtypeStruct((bn, s, D_MODEL), F32),
        compiler_params=pltpu.CompilerParams(
            dimension_semantics=("parallel", "arbitrary"), vmem_limit_bytes=VMEM_LIMIT),
        name="out_proj",
    )(x, ada3, za, zb, ma, mb, pa, pb, wo, fg)


def _rope_tables(s):
    rows = s // GRID_W
    row = jnp.repeat(jnp.arange(rows), GRID_W).astype(F32)
    col = jnp.tile(jnp.arange(GRID_W), rows).astype(F32)
    axis_dim = HEAD_DIM // 2
    inv_freq = ROPE_THETA ** (-jnp.arange(0, axis_dim, 2, dtype=F32) / axis_dim)
    ang_r = row[:, None] * inv_freq
    ang_c = col[:, None] * inv_freq
    cos = jnp.concatenate([jnp.cos(ang_r)] * 2 + [jnp.cos(ang_c)] * 2, axis=-1)
    sin = jnp.concatenate([-jnp.sin(ang_r), jnp.sin(ang_r),
                           -jnp.sin(ang_c), jnp.sin(ang_c)], axis=-1)
    return jnp.tile(cos, (1, 2)), jnp.tile(sin, (1, 2))


def _pair_perm():
    idx = []
    for j in range(A_GROUP):
        for head in (j, A_GROUP + j):
            idx.extend(range(head * HEAD_DIM, (head + 1) * HEAD_DIM))
    return jnp.asarray(idx, jnp.int32)


def _prep_weights(w_in, p_a, p_b, w_out):
    bounds = [0]
    for wdt in (A_WIDTH, A_KV_WIDTH, A_KV_WIDTH, A_WIDTH, B_WIDTH, B_WIDTH, B_WIDTH, B_WIDTH,
                D_MODEL, D_MODEL):
        bounds.append(bounds[-1] + wdt)
    parts = [w_in[:, bounds[i]:bounds[i + 1]] for i in range(10)]
    perm = _pair_perm()
    parts[0] = parts[0][:, perm]
    parts[3] = parts[3][:, perm]
    w_all = jnp.concatenate(parts, axis=1).astype(BF16)
    return w_all, p_a[perm, :].astype(BF16), p_b.astype(BF16), w_out.astype(BF16)


def _trunk(x, c, w_ada, b_ada, norm_g, w_all, gq, gk, lqk, gsub, pa, pb, wo, fg):
    bn = x.shape[0]
    cos, sin = _rope_tables(x.shape[1])
    ada3 = _ada(c, w_ada, b_ada).reshape(bn, 3, D_MODEL)
    qa, ka, va, ga, qb, kb, vb, gb, ma, mb = _in_proj(x, ada3, norm_g, w_all, gq, gk, cos, sin)
    za = _attn_a(qa, ka, va, ga)
    zb = _attn_b(qb, kb, vb, gb, lqk, gsub)
    return _out_proj(x, ada3, za, zb, ma, mb, pa, pb, wo, fg)


def kernel(x_prompt, x_sample, c_prompt, c_sample, w_ada, b_ada, norm_g, w_in, a_q_norm, a_k_norm,
           b_lq1, b_lk1, b_lq2, b_lk2, b_sub_norm, p_a, p_b, w_out, final_g):
    w_all, pa, pb, wo = _prep_weights(w_in[0], p_a[0], p_b[0], w_out[0])
    gq = jnp.tile(a_q_norm[0], 2).reshape(1, LANES)
    gk = jnp.tile(a_k_norm[0], 2).reshape(1, LANES)
    lqk = jnp.stack([b_lq1[0], b_lk1[0], b_lq2[0], b_lk2[0]]).astype(F32)
    gsub = b_sub_norm[0].reshape(1, B_V_DIM)
    fg = final_g.reshape(1, D_MODEL)
    args = (w_ada[0], b_ada[0], norm_g, w_all, gq, gk, lqk, gsub, pa, pb, wo, fg)
    y_prompt = _trunk(x_prompt, c_prompt, *args)
    y_sample = _trunk(x_sample, c_sample, *args)
    return (y_prompt, y_sample)
```

```python
import functools
import math

import jax
import jax.numpy as jnp
from jax import lax
from jax.experimental import pallas as pl
from jax.experimental.pallas import tpu as pltpu

F32 = jnp.float32
BF16 = jnp.bfloat16

D_MODEL = 1024
GRID_W = 64
EPS = 1e-6
ROPE_THETA = 10000.0
A_HEADS = 8
A_KV_HEADS = 2
A_GROUP = A_HEADS // A_KV_HEADS
HEAD_DIM = 64
A_WIDTH = A_HEADS * HEAD_DIM
A_KV_WIDTH = A_KV_HEADS * HEAD_DIM
B_HEADS = 4
B_V_DIM = 2 * HEAD_DIM
B_WIDTH = B_HEADS * B_V_DIM
LAM_INIT = 0.8 - 0.6 * math.exp(-0.3 * 0)
ALIBI_SLOPES = tuple(2.0 ** (-8.0 * h / B_HEADS) for h in range(1, B_HEADS + 1))
QK_SCALE = HEAD_DIM ** -0.5

LANES = 128
SUBLANES = 8
KEY_CHUNK = 256
ONES_ROWS = 16
CHUNK_UNROLL = 8
LOG2E = math.log2(math.e)
VMEM_LIMIT = 56 * 1024 * 1024

_COLS = (("qa", A_WIDTH), ("ka", A_KV_WIDTH), ("va", A_KV_WIDTH), ("ga", A_WIDTH),
         ("qb", B_WIDTH), ("kb", B_WIDTH), ("vb", B_WIDTH), ("gb", B_WIDTH),
         ("ma", D_MODEL), ("mb", D_MODEL))
_OFF = {}
_o = 0
for _n, _w in _COLS:
    _OFF[_n] = (_o, _o + _w)
    _o += _w
D_IN = _o
A_PAIRS = A_WIDTH // LANES
N_SLABS = 4
assert A_PAIRS == N_SLABS and B_HEADS == N_SLABS and B_V_DIM == LANES
SLAB_MAJOR = ("qa", "ga", "qb", "kb", "vb", "gb")

TM = 512
TQ_A = 512
TQ_B = 512


def _dot(a, b):
    return jnp.dot(a, b, preferred_element_type=F32)


def _dot_nt(a, b):
    return lax.dot_general(a, b, (((1,), (1,)), ((), ())), preferred_element_type=F32)


def _split_bf16(x):
    hi = x.astype(BF16)
    lo = (x - hi.astype(F32)).astype(BF16)
    return hi, lo


def _ada_kernel(c_ref, w_ref, b_ref, o_ref):
    c = c_ref[...]
    a = c * jax.nn.sigmoid(c)
    a_hi, a_lo = _split_bf16(a)
    w_hi, w_lo = _split_bf16(w_ref[...])
    acc = _dot(a_hi, w_hi) + (_dot(a_hi, w_lo) + _dot(a_lo, w_hi))
    o_ref[...] = acc + b_ref[...]


def _ada(c, w_ada, b_ada):
    bn = c.shape[0]
    n = w_ada.shape[1]
    tn = 512
    return pl.pallas_call(
        _ada_kernel,
        grid=(n // tn,),
        in_specs=[pl.BlockSpec((bn, D_MODEL), lambda j: (0, 0)),
                  pl.BlockSpec((D_MODEL, tn), lambda j: (0, j)),
                  pl.BlockSpec((1, tn), lambda j: (0, j))],
        out_specs=pl.BlockSpec((bn, tn), lambda j: (0, j)),
        out_shape=jax.ShapeDtypeStruct((bn, n), F32),
        name="ada",
    )(c, w_ada, b_ada.reshape(1, n))


def _swap16(t, low16):
    return jnp.where(low16, pltpu.roll(t, LANES - 16, axis=1), pltpu.roll(t, 16, axis=1))


def _head_norm_rope(y, gain, cos, sin, lo64, low16):
    sq = y * y
    s_all = jnp.sum(sq, axis=-1, keepdims=True)
    s_lo = jnp.sum(jnp.where(lo64, sq, 0.0), axis=-1, keepdims=True)
    r_lo = lax.rsqrt(s_lo * (1.0 / HEAD_DIM) + EPS)
    r_hi = lax.rsqrt((s_all - s_lo) * (1.0 / HEAD_DIM) + EPS)
    t = y * gain
    u = t * cos + _swap16(t, low16) * sin
    return u * jnp.where(lo64, r_lo, r_hi)


def _in_proj_kernel(x_ref, ada_ref, ng_ref, w_ref, gq_ref, gk_ref, cos_ref, sin_ref,
                    qa_ref, ka_ref, va_ref, ga_ref, qb_ref, kb_ref, vb_ref, gb_ref,
                    ma_ref, mb_ref):
    x = x_ref[0]
    ada = ada_ref[0]
    shift, scl = ada[0:1, :], ada[1:2, :]
    ms = jnp.mean(x * x, axis=-1, keepdims=True)
    h = (x * lax.rsqrt(ms + EPS)) * (ng_ref[...] * (1.0 + scl)) + shift
    hb = h.astype(BF16)

    lane = lax.broadcasted_iota(jnp.int32, (1, LANES), 1)
    lo64 = lane < HEAD_DIM
    low16 = (lane % 32) < 16
    cos = cos_ref[...]
    sin = sin_ref[...]

    def proj(name):
        c0, c1 = _OFF[name]
        return _dot(hb, w_ref[:, c0:c1])

    def store_slabs(ref, y):
        y = y.astype(BF16)
        for j in range(N_SLABS):
            ref[0, j] = y[:, j * LANES:(j + 1) * LANES]

    y = proj("qa")
    gq = gq_ref[...]
    for j in range(N_SLABS):
        sl = slice(j * LANES, (j + 1) * LANES)
        u = _head_norm_rope(y[:, sl], gq, cos, sin, lo64, low16)
        qa_ref[0, j] = (u * (QK_SCALE * LOG2E)).astype(BF16)
    y = proj("ka")
    ka_ref[0] = _head_norm_rope(y, gk_ref[...], cos, sin, lo64, low16).astype(BF16)
    va_ref[0] = proj("va").astype(BF16)
    y = proj("ga")
    store_slabs(ga_ref, y * jax.nn.sigmoid(y))
    store_slabs(qb_ref, proj("qb") * (QK_SCALE * LOG2E))
    store_slabs(kb_ref, proj("kb"))
    store_slabs(vb_ref, proj("vb"))
    y = proj("gb")
    store_slabs(gb_ref, y * jax.nn.sigmoid(y))
    ma_ref[0] = jax.nn.sigmoid(proj("ma")).astype(BF16)
    mb_ref[0] = jax.nn.sigmoid(proj("mb")).astype(BF16)


def _in_proj(x, ada3, norm_g, w_all, gq, gk, cos, sin):
    bn, s, _ = x.shape
    const = lambda b, i: (0, 0)
    row = lambda b, i: (b, i, 0)
    slab_spec = pl.BlockSpec((1, N_SLABS, TM, LANES), lambda b, i: (b, 0, i, 0))
    return pl.pallas_call(
        _in_proj_kernel,
        grid=(bn, s // TM),
        in_specs=[pl.BlockSpec((1, TM, D_MODEL), row),
                  pl.BlockSpec((1, 3, D_MODEL), lambda b, i: (b, 0, 0)),
                  pl.BlockSpec((1, D_MODEL), const),
                  pl.BlockSpec((D_MODEL, D_IN), const, pipeline_mode=pl.Buffered(1)),
                  pl.BlockSpec((1, LANES), const),
                  pl.BlockSpec((1, LANES), const),
                  pl.BlockSpec((TM, LANES), lambda b, i: (i, 0)),
                  pl.BlockSpec((TM, LANES), lambda b, i: (i, 0))],
        out_specs=[slab_spec if n in SLAB_MAJOR else pl.BlockSpec((1, TM, w), row)
                   for n, w in _COLS],
        out_shape=[jax.ShapeDtypeStruct((bn, N_SLABS, s, LANES) if n in SLAB_MAJOR else (bn, s, w), BF16)
                   for n, w in _COLS],
        compiler_params=pltpu.CompilerParams(
            dimension_semantics=("parallel", "arbitrary"), vmem_limit_bytes=VMEM_LIMIT),
        name="in_proj",
    )(x, ada3, norm_g, w_all, gq, gk, cos, sin)


def _softmax_rows(s):
    m = jnp.max(s, axis=-1, keepdims=True)
    p = jnp.exp(s - m)
    return p, jnp.sum(p, axis=-1, keepdims=True)


def _col_max(s, m8):
    return jnp.maximum(m8, jnp.max(s.reshape(s.shape[0] // SUBLANES, SUBLANES, s.shape[1]), axis=0))


def _attn_a_kernel(q_ref, k_ref, v_ref, g_ref, o_ref, vt_ref, qm_ref, s0_ref, s1_ref, ot_ref):
    tq = TQ_A
    s_len = k_ref.shape[1]
    n_chunks = s_len // KEY_CHUNK
    n_units = A_HEADS * (s_len // tq)

    vt = v_ref[0].astype(F32).T
    for g in range(A_KV_HEADS):
        for c in range(n_chunks):
            cols = slice(c * KEY_CHUNK, (c + 1) * KEY_CHUNK)
            vt_ref[g, c, 0:HEAD_DIM, :] = vt[g * HEAD_DIM:(g + 1) * HEAD_DIM, cols].astype(BF16)
            vt_ref[g, c, HEAD_DIM:, :] = jnp.ones((ONES_ROWS, KEY_CHUNK), BF16)

    lane = lax.broadcasted_iota(jnp.int32, (1, LANES), 1)
    lo64 = lane < HEAD_DIM
    zero = jnp.zeros((), BF16)
    for n in range(n_units):
        keep = lo64 if n % 2 == 0 else jnp.logical_not(lo64)
        tile = n // A_HEADS
        qm_ref[n] = jnp.where(keep, q_ref[0, (n % A_HEADS) // 2, tile * tq:(tile + 1) * tq, :], zero)

    s_bufs = (s0_ref, s1_ref)
    m8_init = jnp.full((SUBLANES, tq), -jnp.inf, F32)

    def run_stage(kv, m8, n_next):
        m = None if m8 is None else jnp.max(m8, axis=0, keepdims=True)
        qm = None if n_next is None else qm_ref[n_next]
        acc = jnp.zeros((HEAD_DIM + ONES_ROWS, tq), F32)
        m8_next = m8_init
        for c in range(n_chunks):
            rows = slice(c * KEY_CHUNK, (c + 1) * KEY_CHUNK)
            if qm is not None:
                sc = _dot_nt(k_ref[0, rows, :], qm)
                s_bufs[1 - kv][rows, :] = sc
                m8_next = _col_max(sc, m8_next)
            if m is not None:
                p = jnp.exp2(s_bufs[kv][rows, :] - m).astype(BF16)
                acc = acc + _dot(vt_ref[kv, c], p)
        if m is not None:
            ot_ref[kv * HEAD_DIM:(kv + 1) * HEAD_DIM, :] = (
                acc[0:HEAD_DIM] * (1.0 / acc[HEAD_DIM:HEAD_DIM + 1]))
        return m8_next

    def emit_pair(n):
        j = lax.shift_right_logical(n, 1) & (A_PAIRS - 1)
        rows = pl.ds(pl.multiple_of(lax.shift_right_logical(n, 3) * tq, tq), tq)
        o = ot_ref[...].T
        o_ref[0, j, rows, :] = (o * g_ref[0, j, rows, :].astype(F32)).astype(BF16)

    def even_stage(n, m8):
        return run_stage(0, m8, n + 1)

    def odd_stage(n, m8):
        m8_next = run_stage(1, m8, n + 1)
        emit_pair(n)
        return m8_next

    def stage(n, m8):
        return lax.cond((n & 1) == 0, functools.partial(even_stage, n),
                        functools.partial(odd_stage, n), m8)

    m8 = lax.fori_loop(0, n_units - 1, stage, run_stage(1, None, 0))
    run_stage(1, m8, None)
    emit_pair(jnp.int32(n_units - 1))


def _attn_a(qa, ka, va, ga):
    bn, _, s, _ = qa.shape
    slab = lambda b: (b, 0, 0, 0)
    full = lambda b: (b, 0, 0)
    return pl.pallas_call(
        _attn_a_kernel,
        grid=(bn,),
        in_specs=[pl.BlockSpec((1, A_PAIRS, s, LANES), slab),
                  pl.BlockSpec((1, s, A_KV_WIDTH), full),
                  pl.BlockSpec((1, s, A_KV_WIDTH), full),
                  pl.BlockSpec((1, A_PAIRS, s, LANES), slab)],
        out_specs=pl.BlockSpec((1, A_PAIRS, s, LANES), slab),
        out_shape=jax.ShapeDtypeStruct((bn, A_PAIRS, s, LANES), BF16),
        scratch_shapes=[pltpu.VMEM((A_KV_HEADS, s // KEY_CHUNK, HEAD_DIM + ONES_ROWS, KEY_CHUNK), BF16),
                        pltpu.VMEM((A_HEADS * (s // TQ_A), TQ_A, LANES), BF16),
                        pltpu.VMEM((s, TQ_A), F32),
                        pltpu.VMEM((s, TQ_A), F32),
                        pltpu.VMEM((LANES, TQ_A), F32)],
        compiler_params=pltpu.CompilerParams(
            dimension_semantics=("parallel",), vmem_limit_bytes=VMEM_LIMIT),
        name="attn_a",
    )(qa, ka, va, ga)


def _attn_b_kernel(slope_ref, q_ref, k_ref, v_ref, g_ref, lqk_ref, gsub_ref, o_ref,
                   vt_ref, qm_ref, s0_ref, s1_ref, dist_ref, o1_ref):
    tq = TQ_B
    s_len = k_ref.shape[2]
    n_chunks = s_len // KEY_CHUNK
    n_tiles = s_len // tq
    n_units = 2 * B_HEADS * n_tiles
    dist_pad = (n_tiles - 1) * tq

    for h in range(B_HEADS):
        vt = v_ref[0, h].astype(F32).T
        for c in range(n_chunks):
            cols = slice(c * KEY_CHUNK, (c + 1) * KEY_CHUNK)
            vt_ref[h, c, 0:B_V_DIM, :] = vt[:, cols].astype(BF16)
            vt_ref[h, c, B_V_DIM:, :] = jnp.ones((ONES_ROWS, KEY_CHUNK), BF16)

    lqk = lqk_ref[...]
    lam = (jnp.exp(jnp.sum(lqk[0:1] * lqk[1:2], axis=-1, keepdims=True))
           - jnp.exp(jnp.sum(lqk[2:3] * lqk[3:4], axis=-1, keepdims=True)) + LAM_INIT)
    lane = lax.broadcasted_iota(jnp.int32, (1, LANES), 1)
    lo64 = lane < HEAD_DIM
    zero = jnp.zeros((), BF16)
    for n in range(n_units):
        keep = lo64 if n % 2 == 0 else jnp.logical_not(lo64)
        tile = n // (2 * B_HEADS)
        qm_ref[n] = jnp.where(keep, q_ref[0, (n // 2) % B_HEADS, tile * tq:(tile + 1) * tq, :], zero)

    rel = (lax.broadcasted_iota(jnp.int32, (KEY_CHUNK, tq), 0)
           - lax.broadcasted_iota(jnp.int32, (KEY_CHUNK, tq), 1))
    for r in range((s_len + dist_pad) // KEY_CHUNK):
        dist_ref[r * KEY_CHUNK:(r + 1) * KEY_CHUNK, :] = (
            jnp.abs(rel + (r * KEY_CHUNK - dist_pad)).astype(F32))

    s_bufs = (s0_ref, s1_ref)
    m8_init = jnp.full((SUBLANES, tq), -jnp.inf, F32)

    def run_stage(par, m8, n_fin, n_next):
        acc = jnp.zeros((B_V_DIM + ONES_ROWS, tq), F32)
        m8_next = m8_init
        if m8 is not None:
            m = jnp.max(m8, axis=0, keepdims=True)
            h_fin = lax.shift_right_logical(n_fin, 1) & (B_HEADS - 1)
        if n_next is not None:
            qm = qm_ref[n_next]
            h_next = lax.shift_right_logical(n_next, 1) & (B_HEADS - 1)
            slope = slope_ref[h_next]
            dist0 = dist_pad - lax.shift_right_logical(n_next, 3) * tq
        for c in range(n_chunks):
            rows = slice(c * KEY_CHUNK, (c + 1) * KEY_CHUNK)
            if n_next is not None:
                drows = pl.ds(pl.multiple_of(dist0 + c * KEY_CHUNK, KEY_CHUNK), KEY_CHUNK)
                sc = _dot_nt(k_ref[0, h_next, rows, :], qm) + dist_ref[drows, :] * slope
                s_bufs[1 - par][rows, :] = sc
                m8_next = _col_max(sc, m8_next)
            if m8 is not None:
                p = jnp.exp2(s_bufs[par][rows, :] - m).astype(BF16)
                acc = acc + _dot(vt_ref[h_fin, c], p)
        if m8 is not None:
            o = acc[0:B_V_DIM] * (1.0 / acc[B_V_DIM:B_V_DIM + 1])
            if par == 0:
                o1_ref[...] = o
            else:
                o = o1_ref[...] - lam * o
                o = o * lax.rsqrt(jnp.mean(o * o, axis=0, keepdims=True) + EPS)
                o = o.T * (gsub_ref[...] * (1.0 - LAM_INIT))
                orows = pl.ds(pl.multiple_of(lax.shift_right_logical(n_fin, 3) * tq, tq), tq)
                o_ref[0, h_fin, orows, :] = (o * g_ref[0, h_fin, orows, :].astype(F32)).astype(BF16)
        return m8_next

    def stage(n, m8):
        return lax.cond((n & 1) == 0,
                        lambda m: run_stage(0, m, n, n + 1),
                        lambda m: run_stage(1, m, n, n + 1), m8)

    m8 = lax.fori_loop(0, n_units - 1, stage, run_stage(1, None, None, jnp.int32(0)))
    run_stage(1, m8, jnp.int32(n_units - 1), None)


def _attn_b(qb, kb, vb, gb, lqk, gsub):
    bn, _, s, _ = qb.shape
    full = lambda b: (b, 0, 0, 0)
    const = lambda b: (0, 0)
    slopes = jnp.asarray([-sl * LOG2E for sl in ALIBI_SLOPES], F32)
    return pl.pallas_call(
        _attn_b_kernel,
        grid=(bn,),
        in_specs=[pl.BlockSpec(memory_space=pltpu.SMEM),
                  pl.BlockSpec((1, B_HEADS, s, LANES), full),
                  pl.BlockSpec((1, B_HEADS, s, LANES), full),
                  pl.BlockSpec((1, B_HEADS, s, LANES), full),
                  pl.BlockSpec((1, B_HEADS, s, LANES), full),
                  pl.BlockSpec((4, HEAD_DIM), const),
                  pl.BlockSpec((1, B_V_DIM), const)],
        out_specs=pl.BlockSpec((1, B_HEADS, s, LANES), full),
        out_shape=jax.ShapeDtypeStruct((bn, B_HEADS, s, LANES), BF16),
        scratch_shapes=[pltpu.VMEM((B_HEADS, s // KEY_CHUNK, B_V_DIM + ONES_ROWS, KEY_CHUNK), BF16),
                        pltpu.VMEM((2 * B_HEADS * (s // TQ_B), TQ_B, LANES), BF16),
                        pltpu.VMEM((s, TQ_B), F32),
                        pltpu.VMEM((s, TQ_B), F32),
                        pltpu.VMEM((2 * s - TQ_B, TQ_B), F32),
                        pltpu.VMEM((B_V_DIM, TQ_B), F32)],
        compiler_params=pltpu.CompilerParams(
            dimension_semantics=("parallel",), vmem_limit_bytes=VMEM_LIMIT),
        name="attn_b",
    )(slopes, qb, kb, vb, gb, lqk, gsub)


def _out_proj_kernel(x_ref, ada_ref, za_ref, zb_ref, ma_ref, mb_ref,
                     pa_ref, pb_ref, wo_ref, fg_ref, y_ref):
    gate = ada_ref[0][2:3, :]
    za = jnp.concatenate([za_ref[0, j] for j in range(N_SLABS)], axis=1)
    zb = jnp.concatenate([zb_ref[0, j] for j in range(N_SLABS)], axis=1)
    ya = _dot(za, pa_ref[...])
    yb = _dot(zb, pb_ref[...])
    merged = ma_ref[0].astype(F32) * ya + mb_ref[0].astype(F32) * yb
    out = x_ref[0] + gate * _dot(merged.astype(BF16), wo_ref[...])
    r = lax.rsqrt(jnp.mean(out * out, axis=-1, keepdims=True) + EPS)
    y_ref[0] = out * r * fg_ref[...]


def _out_proj(x, ada3, za, zb, ma, mb, pa, pb, wo, fg):
    bn, s, _ = x.shape
    const = lambda b, i: (0, 0)
    row = lambda b, i: (b, i, 0)
    return pl.pallas_call(
        _out_proj_kernel,
        grid=(bn, s // TM),
        in_specs=[pl.BlockSpec((1, TM, D_MODEL), row),
                  pl.BlockSpec((1, 3, D_MODEL), lambda b, i: (b, 0, 0)),
                  pl.BlockSpec((1, N_SLABS, TM, LANES), lambda b, i: (b, 0, i, 0)),
                  pl.BlockSpec((1, N_SLABS, TM, LANES), lambda b, i: (b, 0, i, 0)),
                  pl.BlockSpec((1, TM, D_MODEL), row),
                  pl.BlockSpec((1, TM, D_MODEL), row),
                  pl.BlockSpec((A_WIDTH, D_MODEL), const),
                  pl.BlockSpec((B_WIDTH, D_MODEL), const),
                  pl.BlockSpec((D_MODEL, D_MODEL), const),
                  pl.BlockSpec((1, D_MODEL), const)],
        out_specs=pl.BlockSpec((1, TM, D_MODEL), row),
        out_shape=jax.ShapeDtypeStruct((bn, s, D_MODEL), F32),
        compiler_params=pltpu.CompilerParams(
            dimension_semantics=("parallel", "arbitrary"), vmem_limit_bytes=VMEM_LIMIT),
        name="out_proj",
    )(x, ada3, za, zb, ma, mb, pa, pb, wo, fg)


def _rope_tables(s):
    rows = s // GRID_W
    row = jnp.repeat(jnp.arange(rows), GRID_W).astype(F32)
    col = jnp.tile(jnp.arange(GRID_W), rows).astype(F32)
    axis_dim = HEAD_DIM // 2
    inv_freq = ROPE_THETA ** (-jnp.arange(0, axis_dim, 2, dtype=F32) / axis_dim)
    ang_r = row[:, None] * inv_freq
    ang_c = col[:, None] * inv_freq
    cos = jnp.concatenate([jnp.cos(ang_r)] * 2 + [jnp.cos(ang_c)] * 2, axis=-1)
    sin = jnp.concatenate([-jnp.sin(ang_r), jnp.sin(ang_r),
                           -jnp.sin(ang_c), jnp.sin(ang_c)], axis=-1)
    return jnp.tile(cos, (1, 2)), jnp.tile(sin, (1, 2))


def _pair_perm():
    idx = []
    for j in range(A_GROUP):
        for head in (j, A_GROUP + j):
            idx.extend(range(head * HEAD_DIM, (head + 1) * HEAD_DIM))
    return jnp.asarray(idx, jnp.int32)


def _prep_weights(w_in, p_a, p_b, w_out):
    bounds = [0]
    for wdt in (A_WIDTH, A_KV_WIDTH, A_KV_WIDTH, A_WIDTH, B_WIDTH, B_WIDTH, B_WIDTH, B_WIDTH,
                D_MODEL, D_MODEL):
        bounds.append(bounds[-1] + wdt)
    parts = [w_in[:, bounds[i]:bounds[i + 1]] for i in range(10)]
    perm = _pair_perm()
    parts[0] = parts[0][:, perm]
    parts[3] = parts[3][:, perm]
    w_all = jnp.concatenate(parts, axis=1).astype(BF16)
    return w_all, p_a[perm, :].astype(BF16), p_b.astype(BF16), w_out.astype(BF16)


def _trunk(x, c, w_ada, b_ada, norm_g, w_all, gq, gk, lqk, gsub, pa, pb, wo, fg):
    bn = x.shape[0]
    cos, sin = _rope_tables(x.shape[1])
    ada3 = _ada(c, w_ada, b_ada).reshape(bn, 3, D_MODEL)
    qa, ka, va, ga, qb, kb, vb, gb, ma, mb = _in_proj(x, ada3, norm_g, w_all, gq, gk, cos, sin)
    za = _attn_a(qa, ka, va, ga)
    zb = _attn_b(qb, kb, vb, gb, lqk, gsub)
    return _out_proj(x, ada3, za, zb, ma, mb, pa, pb, wo, fg)


def kernel(x_prompt, x_sample, c_prompt, c_sample, w_ada, b_ada, norm_g, w_in, a_q_norm, a_k_norm,
           b_lq1, b_lk1, b_lq2, b_lk2, b_sub_norm, p_a, p_b, w_out, final_g):
    w_all, pa, pb, wo = _prep_weights(w_in[0], p_a[0], p_b[0], w_out[0])
    gq = jnp.tile(a_q_norm[0], 2).reshape(1, LANES)
    gk = jnp.tile(a_k_norm[0], 2).reshape(1, LANES)
    lqk = jnp.stack([b_lq1[0], b_lk1[0], b_lq2[0], b_lk2[0]]).astype(F32)
    gsub = b_sub_norm[0].reshape(1, B_V_DIM)
    fg = final_g.reshape(1, D_MODEL)
    args = (w_ada[0], b_ada[0], norm_g, w_all, gq, gk, lqk, gsub, pa, pb, wo, fg)
    y_prompt = _trunk(x_prompt, c_prompt, *args)
    y_sample = _trunk(x_sample, c_sample, *args)
    return (y_prompt, y_sample)
```

```python
import functools
import math

import jax
import jax.numpy as jnp
from jax import lax
from jax.experimental import pallas as pl
from jax.experimental.pallas import tpu as pltpu

F32 = jnp.float32
BF16 = jnp.bfloat16

D_MODEL = 1024
GRID_W = 64
EPS = 1e-6
ROPE_THETA = 10000.0
A_HEADS = 8
A_KV_HEADS = 2
A_GROUP = A_HEADS // A_KV_HEADS
HEAD_DIM = 64
A_WIDTH = A_HEADS * HEAD_DIM
A_KV_WIDTH = A_KV_HEADS * HEAD_DIM
B_HEADS = 4
B_V_DIM = 2 * HEAD_DIM
B_WIDTH = B_HEADS * B_V_DIM
LAM_INIT = 0.8 - 0.6 * math.exp(-0.3 * 0)
ALIBI_SLOPES = tuple(2.0 ** (-8.0 * h / B_HEADS) for h in range(1, B_HEADS + 1))
QK_SCALE = HEAD_DIM ** -0.5

LANES = 128
SUBLANES = 8
KEY_CHUNK = 256
ONES_ROWS = 16
CHUNK_UNROLL = 8
LOG2E = math.log2(math.e)
VMEM_LIMIT = 56 * 1024 * 1024

_COLS = (("qa", A_WIDTH), ("ka", A_KV_WIDTH), ("va", A_KV_WIDTH), ("ga", A_WIDTH),
         ("qb", B_WIDTH), ("kb", B_WIDTH), ("vb", B_WIDTH), ("gb", B_WIDTH),
         ("ma", D_MODEL), ("mb", D_MODEL))
_OFF = {}
_o = 0
for _n, _w in _COLS:
    _OFF[_n] = (_o, _o + _w)
    _o += _w
D_IN = _o
A_PAIRS = A_WIDTH // LANES
N_SLABS = 4
assert A_PAIRS == N_SLABS and B_HEADS == N_SLABS and B_V_DIM == LANES
SLAB_MAJOR = ("qa", "ga", "qb", "kb", "vb", "gb")

TM = 512
TQ_A = 1024
TQ_B = 512


def _dot(a, b):
    return jnp.dot(a, b, preferred_element_type=F32)


def _dot_nt(a, b):
    return lax.dot_general(a, b, (((1,), (1,)), ((), ())), preferred_element_type=F32)


def _split_bf16(x):
    hi = x.astype(BF16)
    lo = (x - hi.astype(F32)).astype(BF16)
    return hi, lo


def _ada_kernel(c_ref, w_ref, b_ref, o_ref):
    c = c_ref[...]
    a = c * jax.nn.sigmoid(c)
    a_hi, a_lo = _split_bf16(a)
    w_hi, w_lo = _split_bf16(w_ref[...])
    acc = _dot(a_hi, w_hi) + (_dot(a_hi, w_lo) + _dot(a_lo, w_hi))
    o_ref[...] = acc + b_ref[...]


def _ada(c, w_ada, b_ada):
    bn = c.shape[0]
    n = w_ada.shape[1]
    tn = 512
    return pl.pallas_call(
        _ada_kernel,
        grid=(n // tn,),
        in_specs=[pl.BlockSpec((bn, D_MODEL), lambda j: (0, 0)),
                  pl.BlockSpec((D_MODEL, tn), lambda j: (0, j)),
                  pl.BlockSpec((1, tn), lambda j: (0, j))],
        out_specs=pl.BlockSpec((bn, tn), lambda j: (0, j)),
        out_shape=jax.ShapeDtypeStruct((bn, n), F32),
        name="ada",
    )(c, w_ada, b_ada.reshape(1, n))


def _swap16(t, low16):
    return jnp.where(low16, pltpu.roll(t, LANES - 16, axis=1), pltpu.roll(t, 16, axis=1))


def _head_norm_rope(y, gain, cos, sin, lo64, low16):
    sq = y * y
    s_all = jnp.sum(sq, axis=-1, keepdims=True)
    s_lo = jnp.sum(jnp.where(lo64, sq, 0.0), axis=-1, keepdims=True)
    r_lo = lax.rsqrt(s_lo * (1.0 / HEAD_DIM) + EPS)
    r_hi = lax.rsqrt((s_all - s_lo) * (1.0 / HEAD_DIM) + EPS)
    t = y * gain
    u = t * cos + _swap16(t, low16) * sin
    return u * jnp.where(lo64, r_lo, r_hi)


def _in_proj_kernel(x_ref, ada_ref, ng_ref, w_ref, gq_ref, gk_ref, cos_ref, sin_ref,
                    qa_ref, ka_ref, va_ref, ga_ref, qb_ref, kb_ref, vb_ref, gb_ref,
                    ma_ref, mb_ref):
    x = x_ref[0]
    ada = ada_ref[0]
    shift, scl = ada[0:1, :], ada[1:2, :]
    ms = jnp.mean(x * x, axis=-1, keepdims=True)
    h = (x * lax.rsqrt(ms + EPS)) * (ng_ref[...] * (1.0 + scl)) + shift
    hb = h.astype(BF16)

    lane = lax.broadcasted_iota(jnp.int32, (1, LANES), 1)
    lo64 = lane < HEAD_DIM
    low16 = (lane % 32) < 16
    cos = cos_ref[...]
    sin = sin_ref[...]

    def proj(name):
        c0, c1 = _OFF[name]
        return _dot(hb, w_ref[:, c0:c1])

    def store_slabs(ref, y):
        y = y.astype(BF16)
        for j in range(N_SLABS):
            ref[0, j] = y[:, j * LANES:(j + 1) * LANES]

    y = proj("qa")
    gq = gq_ref[...]
    for j in range(N_SLABS):
        sl = slice(j * LANES, (j + 1) * LANES)
        u = _head_norm_rope(y[:, sl], gq, cos, sin, lo64, low16)
        qa_ref[0, j] = (u * (QK_SCALE * LOG2E)).astype(BF16)
    y = proj("ka")
    ka_ref[0] = _head_norm_rope(y, gk_ref[...], cos, sin, lo64, low16).astype(BF16)
    va_ref[0] = proj("va").astype(BF16)
    y = proj("ga")
    store_slabs(ga_ref, y * jax.nn.sigmoid(y))
    store_slabs(qb_ref, proj("qb") * (QK_SCALE * LOG2E))
    store_slabs(kb_ref, proj("kb"))
    store_slabs(vb_ref, proj("vb"))
    y = proj("gb")
    store_slabs(gb_ref, y * jax.nn.sigmoid(y))
    ma_ref[0] = jax.nn.sigmoid(proj("ma")).astype(BF16)
    mb_ref[0] = jax.nn.sigmoid(proj("mb")).astype(BF16)


def _in_proj(x, ada3, norm_g, w_all, gq, gk, cos, sin):
    bn, s, _ = x.shape
    const = lambda b, i: (0, 0)
    row = lambda b, i: (b, i, 0)
    slab_spec = pl.BlockSpec((1, N_SLABS, TM, LANES), lambda b, i: (b, 0, i, 0))
    return pl.pallas_call(
        _in_proj_kernel,
        grid=(bn, s // TM),
        in_specs=[pl.BlockSpec((1, TM, D_MODEL), row),
                  pl.BlockSpec((1, 3, D_MODEL), lambda b, i: (b, 0, 0)),
                  pl.BlockSpec((1, D_MODEL), const),
                  pl.BlockSpec((D_MODEL, D_IN), const, pipeline_mode=pl.Buffered(1)),
                  pl.BlockSpec((1, LANES), const),
                  pl.BlockSpec((1, LANES), const),
                  pl.BlockSpec((TM, LANES), lambda b, i: (i, 0)),
                  pl.BlockSpec((TM, LANES), lambda b, i: (i, 0))],
        out_specs=[slab_spec if n in SLAB_MAJOR else pl.BlockSpec((1, TM, w), row)
                   for n, w in _COLS],
        out_shape=[jax.ShapeDtypeStruct((bn, N_SLABS, s, LANES) if n in SLAB_MAJOR else (bn, s, w), BF16)
                   for n, w in _COLS],
        compiler_params=pltpu.CompilerParams(
            dimension_semantics=("parallel", "arbitrary"), vmem_limit_bytes=VMEM_LIMIT),
        name="in_proj",
    )(x, ada3, norm_g, w_all, gq, gk, cos, sin)


def _softmax_rows(s):
    m = jnp.max(s, axis=-1, keepdims=True)
    p = jnp.exp(s - m)
    return p, jnp.sum(p, axis=-1, keepdims=True)


def _col_max(s, m8):
    return jnp.maximum(m8, jnp.max(s.reshape(s.shape[0] // SUBLANES, SUBLANES, s.shape[1]), axis=0))


def _attn_a_kernel(q_ref, k_ref, v_ref, g_ref, o_ref, vt_ref, qm_ref, s0_ref, s1_ref, ot_ref):
    tq = TQ_A
    s_len = k_ref.shape[1]
    n_chunks = s_len // KEY_CHUNK
    n_units = A_HEADS * (s_len // tq)

    vt = v_ref[0].astype(F32).T
    for g in range(A_KV_HEADS):
        for c in range(n_chunks):
            cols = slice(c * KEY_CHUNK, (c + 1) * KEY_CHUNK)
            vt_ref[g, c, 0:HEAD_DIM, :] = vt[g * HEAD_DIM:(g + 1) * HEAD_DIM, cols].astype(BF16)
            vt_ref[g, c, HEAD_DIM:, :] = jnp.ones((ONES_ROWS, KEY_CHUNK), BF16)

    lane = lax.broadcasted_iota(jnp.int32, (1, LANES), 1)
    lo64 = lane < HEAD_DIM
    zero = jnp.zeros((), BF16)
    for n in range(n_units):
        keep = lo64 if n % 2 == 0 else jnp.logical_not(lo64)
        tile = n // A_HEADS
        qm_ref[n] = jnp.where(keep, q_ref[0, (n % A_HEADS) // 2, tile * tq:(tile + 1) * tq, :], zero)

    s_bufs = (s0_ref, s1_ref)
    m8_init = jnp.full((SUBLANES, tq), -jnp.inf, F32)

    def run_stage(kv, m8, n_next):
        m = None if m8 is None else jnp.max(m8, axis=0, keepdims=True)
        qm = None if n_next is None else qm_ref[n_next]
        acc = jnp.zeros((HEAD_DIM + ONES_ROWS, tq), F32)
        m8_next = m8_init
        for c in range(n_chunks):
            rows = slice(c * KEY_CHUNK, (c + 1) * KEY_CHUNK)
            if qm is not None:
                sc = _dot_nt(k_ref[0, rows, :], qm)
                s_bufs[1 - kv][rows, :] = sc
                m8_next = _col_max(sc, m8_next)
            if m is not None:
                p = jnp.exp2(s_bufs[kv][rows, :] - m).astype(BF16)
                acc = acc + _dot(vt_ref[kv, c], p)
        if m is not None:
            ot_ref[kv * HEAD_DIM:(kv + 1) * HEAD_DIM, :] = (
                acc[0:HEAD_DIM] * (1.0 / acc[HEAD_DIM:HEAD_DIM + 1]))
        return m8_next

    def emit_pair(n):
        j = lax.shift_right_logical(n, 1) & (A_PAIRS - 1)
        rows = pl.ds(pl.multiple_of(lax.shift_right_logical(n, 3) * tq, tq), tq)
        o = ot_ref[...].T
        o_ref[0, j, rows, :] = (o * g_ref[0, j, rows, :].astype(F32)).astype(BF16)

    def even_stage(n, m8):
        emit_pair(n - 1)
        return run_stage(0, m8, n + 1)

    def odd_stage(n, m8):
        return run_stage(1, m8, n + 1)

    def stage(n, m8):
        return lax.cond((n & 1) == 0, functools.partial(even_stage, n),
                        functools.partial(odd_stage, n), m8)

    m8 = run_stage(0, run_stage(1, None, 0), 1)
    m8 = lax.fori_loop(1, n_units - 1, stage, m8)
    run_stage(1, m8, None)
    emit_pair(jnp.int32(n_units - 1))


def _attn_a(qa, ka, va, ga):
    bn, _, s, _ = qa.shape
    slab = lambda b: (b, 0, 0, 0)
    full = lambda b: (b, 0, 0)
    return pl.pallas_call(
        _attn_a_kernel,
        grid=(bn,),
        in_specs=[pl.BlockSpec((1, A_PAIRS, s, LANES), slab),
                  pl.BlockSpec((1, s, A_KV_WIDTH), full),
                  pl.BlockSpec((1, s, A_KV_WIDTH), full),
                  pl.BlockSpec((1, A_PAIRS, s, LANES), slab)],
        out_specs=pl.BlockSpec((1, A_PAIRS, s, LANES), slab),
        out_shape=jax.ShapeDtypeStruct((bn, A_PAIRS, s, LANES), BF16),
        scratch_shapes=[pltpu.VMEM((A_KV_HEADS, s // KEY_CHUNK, HEAD_DIM + ONES_ROWS, KEY_CHUNK), BF16),
                        pltpu.VMEM((A_HEADS * (s // TQ_A), TQ_A, LANES), BF16),
                        pltpu.VMEM((s, TQ_A), F32),
                        pltpu.VMEM((s, TQ_A), F32),
                        pltpu.VMEM((LANES, TQ_A), F32)],
        compiler_params=pltpu.CompilerParams(
            dimension_semantics=("parallel",), vmem_limit_bytes=VMEM_LIMIT),
        name="attn_a",
    )(qa, ka, va, ga)


def _attn_b_kernel(slope_ref, q_ref, k_ref, v_ref, g_ref, lqk_ref, gsub_ref, o_ref,
                   vt_ref, qm_ref, s0_ref, s1_ref, dist_ref, o1_ref, o2_ref):
    tq = TQ_B
    s_len = k_ref.shape[2]
    n_chunks = s_len // KEY_CHUNK
    n_tiles = s_len // tq
    n_units = 2 * B_HEADS * n_tiles
    dist_pad = (n_tiles - 1) * tq

    for h in range(B_HEADS):
        vt = v_ref[0, h].astype(F32).T
        for c in range(n_chunks):
            cols = slice(c * KEY_CHUNK, (c + 1) * KEY_CHUNK)
            vt_ref[h, c, 0:B_V_DIM, :] = vt[:, cols].astype(BF16)
            vt_ref[h, c, B_V_DIM:, :] = jnp.ones((ONES_ROWS, KEY_CHUNK), BF16)

    lqk = lqk_ref[...]
    lam = (jnp.exp(jnp.sum(lqk[0:1] * lqk[1:2], axis=-1, keepdims=True))
           - jnp.exp(jnp.sum(lqk[2:3] * lqk[3:4], axis=-1, keepdims=True)) + LAM_INIT)
    lane = lax.broadcasted_iota(jnp.int32, (1, LANES), 1)
    lo64 = lane < HEAD_DIM
    zero = jnp.zeros((), BF16)
    for n in range(n_units):
        keep = lo64 if n % 2 == 0 else jnp.logical_not(lo64)
        tile = n // (2 * B_HEADS)
        qm_ref[n] = jnp.where(keep, q_ref[0, (n // 2) % B_HEADS, tile * tq:(tile + 1) * tq, :], zero)

    rel = (lax.broadcasted_iota(jnp.int32, (KEY_CHUNK, tq), 0)
           - lax.broadcasted_iota(jnp.int32, (KEY_CHUNK, tq), 1))
    @pl.when(pl.program_id(0) == 0)
    def _():
        for r in range((s_len + dist_pad) // KEY_CHUNK):
            dist_ref[r * KEY_CHUNK:(r + 1) * KEY_CHUNK, :] = (
                jnp.abs(rel + (r * KEY_CHUNK - dist_pad)).astype(F32))

    s_bufs = (s0_ref, s1_ref)
    m8_init = jnp.full((SUBLANES, tq), -jnp.inf, F32)

    def run_stage(par, m8, n_fin, n_next):
        acc = jnp.zeros((B_V_DIM + ONES_ROWS, tq), F32)
        m8_next = m8_init
        if m8 is not None:
            m = jnp.max(m8, axis=0, keepdims=True)
            h_fin = lax.shift_right_logical(n_fin, 1) & (B_HEADS - 1)
        if n_next is not None:
            qm = qm_ref[n_next]
            h_next = lax.shift_right_logical(n_next, 1) & (B_HEADS - 1)
            slope = slope_ref[h_next]
            dist0 = dist_pad - lax.shift_right_logical(n_next, 3) * tq
        for c in range(n_chunks):
            rows = slice(c * KEY_CHUNK, (c + 1) * KEY_CHUNK)
            if n_next is not None:
                drows = pl.ds(pl.multiple_of(dist0 + c * KEY_CHUNK, KEY_CHUNK), KEY_CHUNK)
                sc = _dot_nt(k_ref[0, h_next, rows, :], qm) + dist_ref[drows, :] * slope
                s_bufs[1 - par][rows, :] = sc
                m8_next = _col_max(sc, m8_next)
            if m8 is not None:
                p = jnp.exp2(s_bufs[par][rows, :] - m).astype(BF16)
                acc = acc + _dot(vt_ref[h_fin, c], p)
        if m8 is not None:
            (o1_ref, o2_ref)[par][...] = acc[0:B_V_DIM] * (1.0 / acc[B_V_DIM:B_V_DIM + 1])
        return m8_next

    def emit_head(n):
        h = lax.shift_right_logical(n, 1) & (B_HEADS - 1)
        rows = pl.ds(pl.multiple_of(lax.shift_right_logical(n, 3) * tq, tq), tq)
        o = o1_ref[...] - lam * o2_ref[...]
        o = o * lax.rsqrt(jnp.mean(o * o, axis=0, keepdims=True) + EPS)
        o = o.T * (gsub_ref[...] * (1.0 - LAM_INIT))
        o_ref[0, h, rows, :] = (o * g_ref[0, h, rows, :].astype(F32)).astype(BF16)

    def even_stage(n, m8):
        emit_head(n - 1)
        return run_stage(0, m8, n, n + 1)

    def odd_stage(n, m8):
        return run_stage(1, m8, n, n + 1)

    def stage(n, m8):
        return lax.cond((n & 1) == 0, functools.partial(even_stage, n),
                        functools.partial(odd_stage, n), m8)

    m8 = run_stage(1, None, None, jnp.int32(0))
    m8 = run_stage(0, m8, jnp.int32(0), jnp.int32(1))
    m8 = lax.fori_loop(1, n_units - 1, stage, m8)
    run_stage(1, m8, jnp.int32(n_units - 1), None)
    emit_head(jnp.int32(n_units - 1))


def _attn_b(qb, kb, vb, gb, lqk, gsub):
    bn, _, s, _ = qb.shape
    full = lambda b: (b, 0, 0, 0)
    const = lambda b: (0, 0)
    slopes = jnp.asarray([-sl * LOG2E for sl in ALIBI_SLOPES], F32)
    return pl.pallas_call(
        _attn_b_kernel,
        grid=(bn,),
        in_specs=[pl.BlockSpec(memory_space=pltpu.SMEM),
                  pl.BlockSpec((1, B_HEADS, s, LANES), full),
                  pl.BlockSpec((1, B_HEADS, s, LANES), full),
                  pl.BlockSpec((1, B_HEADS, s, LANES), full),
                  pl.BlockSpec((1, B_HEADS, s, LANES), full),
                  pl.BlockSpec((4, HEAD_DIM), const),
                  pl.BlockSpec((1, B_V_DIM), const)],
        out_specs=pl.BlockSpec((1, B_HEADS, s, LANES), full),
        out_shape=jax.ShapeDtypeStruct((bn, B_HEADS, s, LANES), BF16),
        scratch_shapes=[pltpu.VMEM((B_HEADS, s // KEY_CHUNK, B_V_DIM + ONES_ROWS, KEY_CHUNK), BF16),
                        pltpu.VMEM((2 * B_HEADS * (s // TQ_B), TQ_B, LANES), BF16),
                        pltpu.VMEM((s, TQ_B), F32),
                        pltpu.VMEM((s, TQ_B), F32),
                        pltpu.VMEM((2 * s - TQ_B, TQ_B), F32),
                        pltpu.VMEM((B_V_DIM, TQ_B), F32),
                        pltpu.VMEM((B_V_DIM, TQ_B), F32)],
        compiler_params=pltpu.CompilerParams(
            dimension_semantics=("arbitrary",), vmem_limit_bytes=VMEM_LIMIT),
        name="attn_b",
    )(slopes, qb, kb, vb, gb, lqk, gsub)


def _out_proj_kernel(x_ref, ada_ref, za_ref, zb_ref, ma_ref, mb_ref,
                     pa_ref, pb_ref, wo_ref, fg_ref, y_ref):
    gate = ada_ref[0][2:3, :]
    za = jnp.concatenate([za_ref[0, j] for j in range(N_SLABS)], axis=1)
    zb = jnp.concatenate([zb_ref[0, j] for j in range(N_SLABS)], axis=1)
    ya = _dot(za, pa_ref[...])
    yb = _dot(zb, pb_ref[...])
    merged = ma_ref[0].astype(F32) * ya + mb_ref[0].astype(F32) * yb
    out = x_ref[0] + gate * _dot(merged.astype(BF16), wo_ref[...])
    r = lax.rsqrt(jnp.mean(out * out, axis=-1, keepdims=True) + EPS)
    y_ref[0] = out * r * fg_ref[...]


def _out_proj(x, ada3, za, zb, ma, mb, pa, pb, wo, fg):
    bn, s, _ = x.shape
    const = lambda b, i: (0, 0)
    row = lambda b, i: (b, i, 0)
    return pl.pallas_call(
        _out_proj_kernel,
        grid=(bn, s // TM),
        in_specs=[pl.BlockSpec((1, TM, D_MODEL), row),
                  pl.BlockSpec((1, 3, D_MODEL), lambda b, i: (b, 0, 0)),
                  pl.BlockSpec((1, N_SLABS, TM, LANES), lambda b, i: (b, 0, i, 0)),
                  pl.BlockSpec((1, N_SLABS, TM, LANES), lambda b, i: (b, 0, i, 0)),
                  pl.BlockSpec((1, TM, D_MODEL), row),
                  pl.BlockSpec((1, TM, D_MODEL), row),
                  pl.BlockSpec((A_WIDTH, D_MODEL), const),
                  pl.BlockSpec((B_WIDTH, D_MODEL), const),
                  pl.BlockSpec((D_MODEL, D_MODEL), const),
                  pl.BlockSpec((1, D_MODEL), const)],
        out_specs=pl.BlockSpec((1, TM, D_MODEL), row),
        out_shape=jax.ShapeDtypeStruct((bn, s, D_MODEL), F32),
        compiler_params=pltpu.CompilerParams(
            dimension_semantics=("parallel", "arbitrary"), vmem_limit_bytes=VMEM_LIMIT),
        name="out_proj",
    )(x, ada3, za, zb, ma, mb, pa, pb, wo, fg)


def _rope_tables(s):
    rows = s // GRID_W
    row = jnp.repeat(jnp.arange(rows), GRID_W).astype(F32)
    col = jnp.tile(jnp.arange(GRID_W), rows).astype(F32)
    axis_dim = HEAD_DIM // 2
    inv_freq = ROPE_THETA ** (-jnp.arange(0, axis_dim, 2, dtype=F32) / axis_dim)
    ang_r = row[:, None] * inv_freq
    ang_c = col[:, None] * inv_freq
    cos = jnp.concatenate([jnp.cos(ang_r)] * 2 + [jnp.cos(ang_c)] * 2, axis=-1)
    sin = jnp.concatenate([-jnp.sin(ang_r), jnp.sin(ang_r),
                           -jnp.sin(ang_c), jnp.sin(ang_c)], axis=-1)
    return jnp.tile(cos, (1, 2)), jnp.tile(sin, (1, 2))


def _pair_perm():
    idx = []
    for j in range(A_GROUP):
        for head in (j, A_GROUP + j):
            idx.extend(range(head * HEAD_DIM, (head + 1) * HEAD_DIM))
    return jnp.asarray(idx, jnp.int32)


def _prep_weights(w_in, p_a, p_b, w_out):
    bounds = [0]
    for wdt in (A_WIDTH, A_KV_WIDTH, A_KV_WIDTH, A_WIDTH, B_WIDTH, B_WIDTH, B_WIDTH, B_WIDTH,
                D_MODEL, D_MODEL):
        bounds.append(bounds[-1] + wdt)
    parts = [w_in[:, bounds[i]:bounds[i + 1]] for i in range(10)]
    perm = _pair_perm()
    parts[0] = parts[0][:, perm]
    parts[3] = parts[3][:, perm]
    w_all = jnp.concatenate(parts, axis=1).astype(BF16)
    return w_all, p_a[perm, :].astype(BF16), p_b.astype(BF16), w_out.astype(BF16)


def _trunk(x, c, w_ada, b_ada, norm_g, w_all, gq, gk, lqk, gsub, pa, pb, wo, fg):
    bn = x.shape[0]
    cos, sin = _rope_tables(x.shape[1])
    ada3 = _ada(c, w_ada, b_ada).reshape(bn, 3, D_MODEL)
    qa, ka, va, ga, qb, kb, vb, gb, ma, mb = _in_proj(x, ada3, norm_g, w_all, gq, gk, cos, sin)
    za = _attn_a(qa, ka, va, ga)
    zb = _attn_b(qb, kb, vb, gb, lqk, gsub)
    return _out_proj(x, ada3, za, zb, ma, mb, pa, pb, wo, fg)


def kernel(x_prompt, x_sample, c_prompt, c_sample, w_ada, b_ada, norm_g, w_in, a_q_norm, a_k_norm,
           b_lq1, b_lk1, b_lq2, b_lk2, b_sub_norm, p_a, p_b, w_out, final_g):
    w_all, pa, pb, wo = _prep_weights(w_in[0], p_a[0], p_b[0], w_out[0])
    gq = jnp.tile(a_q_norm[0], 2).reshape(1, LANES)
    gk = jnp.tile(a_k_norm[0], 2).reshape(1, LANES)
    lqk = jnp.stack([b_lq1[0], b_lk1[0], b_lq2[0], b_lk2[0]]).astype(F32)
    gsub = b_sub_norm[0].reshape(1, B_V_DIM)
    fg = final_g.reshape(1, D_MODEL)
    args = (w_ada[0], b_ada[0], norm_g, w_all, gq, gk, lqk, gsub, pa, pb, wo, fg)
    y_prompt = _trunk(x_prompt, c_prompt, *args)
    y_sample = _trunk(x_sample, c_sample, *args)
    return (y_prompt, y_sample)
```

```python
import functools
import math

import jax
import jax.numpy as jnp
from jax import lax
from jax.experimental import pallas as pl
from jax.experimental.pallas import tpu as pltpu

F32 = jnp.float32
BF16 = jnp.bfloat16

D_MODEL = 1024
GRID_W = 64
EPS = 1e-6
ROPE_THETA = 10000.0
A_HEADS = 8
A_KV_HEADS = 2
A_GROUP = A_HEADS // A_KV_HEADS
HEAD_DIM = 64
A_WIDTH = A_HEADS * HEAD_DIM
A_KV_WIDTH = A_KV_HEADS * HEAD_DIM
B_HEADS = 4
B_V_DIM = 2 * HEAD_DIM
B_WIDTH = B_HEADS * B_V_DIM
LAM_INIT = 0.8 - 0.6 * math.exp(-0.3 * 0)
ALIBI_SLOPES = tuple(2.0 ** (-8.0 * h / B_HEADS) for h in range(1, B_HEADS + 1))
QK_SCALE = HEAD_DIM ** -0.5

LANES = 128
SUBLANES = 8
KEY_CHUNK = 256
ONES_ROWS = 16
LOG2E = math.log2(math.e)
VMEM_LIMIT = 56 * 1024 * 1024

_COLS = (("qa", A_WIDTH), ("ka", A_KV_WIDTH), ("va", A_KV_WIDTH), ("ga", A_WIDTH),
         ("qb", B_WIDTH), ("kb", B_WIDTH), ("vb", B_WIDTH), ("gb", B_WIDTH),
         ("ma", D_MODEL), ("mb", D_MODEL))
_OFF = {}
_o = 0
for _n, _w in _COLS:
    _OFF[_n] = (_o, _o + _w)
    _o += _w
D_IN = _o
A_PAIRS = A_WIDTH // LANES
N_SLABS = 4
assert A_PAIRS == N_SLABS and B_HEADS == N_SLABS and B_V_DIM == LANES
SLAB_MAJOR = ("qa", "ga", "qb", "kb", "vb", "gb")

TM = 512
TQ_A = 1024
TQ_B = 512


def _dot(a, b):
    return jnp.dot(a, b, preferred_element_type=F32)


def _dot_nt(a, b):
    return lax.dot_general(a, b, (((1,), (1,)), ((), ())), preferred_element_type=F32)


def _split_bf16(x):
    hi = x.astype(BF16)
    lo = (x - hi.astype(F32)).astype(BF16)
    return hi, lo


def _ada_kernel(c_ref, w_ref, b_ref, o_ref):
    c = c_ref[...]
    a = c * jax.nn.sigmoid(c)
    a_hi, a_lo = _split_bf16(a)
    w_hi, w_lo = _split_bf16(w_ref[...])
    acc = _dot(a_hi, w_hi) + (_dot(a_hi, w_lo) + _dot(a_lo, w_hi))
    o_ref[...] = acc + b_ref[...]


def _ada(c, w_ada, b_ada):
    bn = c.shape[0]
    n = w_ada.shape[1]
    tn = 512
    return pl.pallas_call(
        _ada_kernel,
        grid=(n // tn,),
        in_specs=[pl.BlockSpec((bn, D_MODEL), lambda j: (0, 0)),
                  pl.BlockSpec((D_MODEL, tn), lambda j: (0, j)),
                  pl.BlockSpec((1, tn), lambda j: (0, j))],
        out_specs=pl.BlockSpec((bn, tn), lambda j: (0, j)),
        out_shape=jax.ShapeDtypeStruct((bn, n), F32),
        name="ada",
    )(c, w_ada, b_ada.reshape(1, n))


def _swap16(t, low16):
    return jnp.where(low16, pltpu.roll(t, LANES - 16, axis=1), pltpu.roll(t, 16, axis=1))


def _head_norm_rope(y, gain, cos, sin, lo64, low16):
    sq = y * y
    s_all = jnp.sum(sq, axis=-1, keepdims=True)
    s_lo = jnp.sum(jnp.where(lo64, sq, 0.0), axis=-1, keepdims=True)
    r_lo = lax.rsqrt(s_lo * (1.0 / HEAD_DIM) + EPS)
    r_hi = lax.rsqrt((s_all - s_lo) * (1.0 / HEAD_DIM) + EPS)
    t = y * gain
    u = t * cos + _swap16(t, low16) * sin
    return u * jnp.where(lo64, r_lo, r_hi)


def _in_proj_kernel(x_ref, ada_ref, ng_ref, w_ref, gq_ref, gk_ref, cos_ref, sin_ref,
                    qa_ref, ka_ref, va_ref, ga_ref, qb_ref, kb_ref, vb_ref, gb_ref,
                    ma_ref, mb_ref):
    x = x_ref[0]
    ada = ada_ref[0]
    shift, scl = ada[0:1, :], ada[1:2, :]
    ms = jnp.mean(x * x, axis=-1, keepdims=True)
    h = (x * lax.rsqrt(ms + EPS)) * (ng_ref[...] * (1.0 + scl)) + shift
    hb = h.astype(BF16)

    lane = lax.broadcasted_iota(jnp.int32, (1, LANES), 1)
    lo64 = lane < HEAD_DIM
    low16 = (lane % 32) < 16
    cos = cos_ref[...]
    sin = sin_ref[...]

    def proj(name):
        c0, c1 = _OFF[name]
        return _dot(hb, w_ref[:, c0:c1])

    def store_slabs(ref, y):
        y = y.astype(BF16)
        for j in range(N_SLABS):
            ref[0, j] = y[:, j * LANES:(j + 1) * LANES]

    y = proj("qa")
    gq = gq_ref[...]
    for j in range(N_SLABS):
        sl = slice(j * LANES, (j + 1) * LANES)
        u = _head_norm_rope(y[:, sl], gq, cos, sin, lo64, low16)
        qa_ref[0, j] = (u * (QK_SCALE * LOG2E)).astype(BF16)
    y = proj("ka")
    ka_ref[0] = _head_norm_rope(y, gk_ref[...], cos, sin, lo64, low16).astype(BF16)
    va_ref[0] = proj("va").astype(BF16)
    y = proj("ga")
    store_slabs(ga_ref, y * jax.nn.sigmoid(y))
    store_slabs(qb_ref, proj("qb") * (QK_SCALE * LOG2E))
    store_slabs(kb_ref, proj("kb"))
    store_slabs(vb_ref, proj("vb"))
    y = proj("gb")
    store_slabs(gb_ref, y * jax.nn.sigmoid(y))
    ma_ref[0] = jax.nn.sigmoid(proj("ma")).astype(BF16)
    mb_ref[0] = jax.nn.sigmoid(proj("mb")).astype(BF16)


def _in_proj(x, ada3, norm_g, w_all, gq, gk, cos, sin):
    bn, s, _ = x.shape
    const = lambda b, i: (0, 0)
    row = lambda b, i: (b, i, 0)
    slab_spec = pl.BlockSpec((1, N_SLABS, TM, LANES), lambda b, i: (b, 0, i, 0))
    return pl.pallas_call(
        _in_proj_kernel,
        grid=(bn, s // TM),
        in_specs=[pl.BlockSpec((1, TM, D_MODEL), row),
                  pl.BlockSpec((1, 3, D_MODEL), lambda b, i: (b, 0, 0)),
                  pl.BlockSpec((1, D_MODEL), const),
                  pl.BlockSpec((D_MODEL, D_IN), const, pipeline_mode=pl.Buffered(1)),
                  pl.BlockSpec((1, LANES), const),
                  pl.BlockSpec((1, LANES), const),
                  pl.BlockSpec((TM, LANES), lambda b, i: (i, 0)),
                  pl.BlockSpec((TM, LANES), lambda b, i: (i, 0))],
        out_specs=[slab_spec if n in SLAB_MAJOR else pl.BlockSpec((1, TM, w), row)
                   for n, w in _COLS],
        out_shape=[jax.ShapeDtypeStruct((bn, N_SLABS, s, LANES) if n in SLAB_MAJOR else (bn, s, w), BF16)
                   for n, w in _COLS],
        compiler_params=pltpu.CompilerParams(
            dimension_semantics=("parallel", "arbitrary"), vmem_limit_bytes=VMEM_LIMIT),
        name="in_proj",
    )(x, ada3, norm_g, w_all, gq, gk, cos, sin)


def _col_max(s, m8):
    return jnp.maximum(m8, jnp.max(s.reshape(s.shape[0] // SUBLANES, SUBLANES, s.shape[1]), axis=0))


def _attn_a_kernel(q_ref, k_ref, v_ref, g_ref, o_ref, vt_ref, qm_ref, s0_ref, s1_ref, ot_ref):
    tq = TQ_A
    s_len = k_ref.shape[1]
    n_chunks = s_len // KEY_CHUNK
    n_units = A_HEADS * (s_len // tq)

    vt = v_ref[0].astype(F32).T
    for g in range(A_KV_HEADS):
        for c in range(n_chunks):
            cols = slice(c * KEY_CHUNK, (c + 1) * KEY_CHUNK)
            vt_ref[g, c, 0:HEAD_DIM, :] = vt[g * HEAD_DIM:(g + 1) * HEAD_DIM, cols].astype(BF16)
            vt_ref[g, c, HEAD_DIM:, :] = jnp.ones((ONES_ROWS, KEY_CHUNK), BF16)

    lane = lax.broadcasted_iota(jnp.int32, (1, LANES), 1)
    lo64 = lane < HEAD_DIM
    zero = jnp.zeros((), BF16)
    for n in range(n_units):
        keep = lo64 if n % 2 == 0 else jnp.logical_not(lo64)
        tile = n // A_HEADS
        qm_ref[n] = jnp.where(keep, q_ref[0, (n % A_HEADS) // 2, tile * tq:(tile + 1) * tq, :], zero)

    s_bufs = (s0_ref, s1_ref)
    m8_init = jnp.full((SUBLANES, tq), -jnp.inf, F32)

    def run_stage(kv, m8, n_next):
        m = None if m8 is None else jnp.max(m8, axis=0, keepdims=True)
        qm = None if n_next is None else qm_ref[n_next]
        acc = jnp.zeros((HEAD_DIM + ONES_ROWS, tq), F32)
        m8_next = m8_init
        for c in range(n_chunks):
            rows = slice(c * KEY_CHUNK, (c + 1) * KEY_CHUNK)
            if qm is not None:
                sc = _dot_nt(k_ref[0, rows, :], qm)
                s_bufs[1 - kv][rows, :] = sc
                m8_next = _col_max(sc, m8_next)
            if m is not None:
                p = jnp.exp2(s_bufs[kv][rows, :] - m).astype(BF16)
                acc = acc + _dot(vt_ref[kv, c], p)
        if m is not None:
            ot_ref[kv * HEAD_DIM:(kv + 1) * HEAD_DIM, :] = (
                acc[0:HEAD_DIM] * (1.0 / acc[HEAD_DIM:HEAD_DIM + 1]))
        return m8_next

    def emit_pair(n):
        j = lax.shift_right_logical(n, 1) & (A_PAIRS - 1)
        rows = pl.ds(pl.multiple_of(lax.shift_right_logical(n, 3) * tq, tq), tq)
        o = ot_ref[...].T
        o_ref[0, j, rows, :] = (o * g_ref[0, j, rows, :].astype(F32)).astype(BF16)

    def even_stage(n, m8):
        emit_pair(n - 1)
        return run_stage(0, m8, n + 1)

    def odd_stage(n, m8):
        return run_stage(1, m8, n + 1)

    def stage(n, m8):
        return lax.cond((n & 1) == 0, functools.partial(even_stage, n),
                        functools.partial(odd_stage, n), m8)

    m8 = run_stage(0, run_stage(1, None, 0), 1)
    m8 = lax.fori_loop(1, n_units - 1, stage, m8)
    run_stage(1, m8, None)
    emit_pair(jnp.int32(n_units - 1))


def _attn_a(qa, ka, va, ga):
    bn, _, s, _ = qa.shape
    slab = lambda b: (b, 0, 0, 0)
    full = lambda b: (b, 0, 0)
    return pl.pallas_call(
        _attn_a_kernel,
        grid=(bn,),
        in_specs=[pl.BlockSpec((1, A_PAIRS, s, LANES), slab),
                  pl.BlockSpec((1, s, A_KV_WIDTH), full),
                  pl.BlockSpec((1, s, A_KV_WIDTH), full),
                  pl.BlockSpec((1, A_PAIRS, s, LANES), slab)],
        out_specs=pl.BlockSpec((1, A_PAIRS, s, LANES), slab),
        out_shape=jax.ShapeDtypeStruct((bn, A_PAIRS, s, LANES), BF16),
        scratch_shapes=[pltpu.VMEM((A_KV_HEADS, s // KEY_CHUNK, HEAD_DIM + ONES_ROWS, KEY_CHUNK), BF16),
                        pltpu.VMEM((A_HEADS * (s // TQ_A), TQ_A, LANES), BF16),
                        pltpu.VMEM((s, TQ_A), F32),
                        pltpu.VMEM((s, TQ_A), F32),
                        pltpu.VMEM((LANES, TQ_A), F32)],
        compiler_params=pltpu.CompilerParams(
            dimension_semantics=("parallel",), vmem_limit_bytes=VMEM_LIMIT),
        name="attn_a",
    )(qa, ka, va, ga)


def _attn_b_kernel(slope_ref, q_ref, k_ref, v_ref, g_ref, lqk_ref, gsub_ref, o_ref,
                   vt_ref, qm_ref, s0_ref, s1_ref, dist_ref, o1_ref, o2_ref):
    tq = TQ_B
    s_len = k_ref.shape[2]
    n_chunks = s_len // KEY_CHUNK
    n_tiles = s_len // tq
    n_units = 2 * B_HEADS * n_tiles
    dist_pad = (n_tiles - 1) * tq

    for h in range(B_HEADS):
        vt = v_ref[0, h].astype(F32).T
        for c in range(n_chunks):
            cols = slice(c * KEY_CHUNK, (c + 1) * KEY_CHUNK)
            vt_ref[h, c, 0:B_V_DIM, :] = vt[:, cols].astype(BF16)
            vt_ref[h, c, B_V_DIM:, :] = jnp.ones((ONES_ROWS, KEY_CHUNK), BF16)

    lqk = lqk_ref[...]
    lam = (jnp.exp(jnp.sum(lqk[0:1] * lqk[1:2], axis=-1, keepdims=True))
           - jnp.exp(jnp.sum(lqk[2:3] * lqk[3:4], axis=-1, keepdims=True)) + LAM_INIT)
    lane = lax.broadcasted_iota(jnp.int32, (1, LANES), 1)
    lo64 = lane < HEAD_DIM
    zero = jnp.zeros((), BF16)
    for n in range(n_units):
        keep = lo64 if n % 2 == 0 else jnp.logical_not(lo64)
        tile = n // (2 * B_HEADS)
        qm_ref[n] = jnp.where(keep, q_ref[0, (n // 2) % B_HEADS, tile * tq:(tile + 1) * tq, :], zero)

    rel = (lax.broadcasted_iota(jnp.int32, (KEY_CHUNK, tq), 0)
           - lax.broadcasted_iota(jnp.int32, (KEY_CHUNK, tq), 1))
    @pl.when(pl.program_id(0) == 0)
    def _():
        for r in range((s_len + dist_pad) // KEY_CHUNK):
            dist_ref[r * KEY_CHUNK:(r + 1) * KEY_CHUNK, :] = (
                jnp.abs(rel + (r * KEY_CHUNK - dist_pad)).astype(F32))

    s_bufs = (s0_ref, s1_ref)
    m8_init = jnp.full((SUBLANES, tq), -jnp.inf, F32)

    def run_stage(par, m8, n_fin, n_next):
        acc = jnp.zeros((B_V_DIM + ONES_ROWS, tq), F32)
        m8_next = m8_init
        if m8 is not None:
            m = jnp.max(m8, axis=0, keepdims=True)
            h_fin = lax.shift_right_logical(n_fin, 1) & (B_HEADS - 1)
        if n_next is not None:
            qm = qm_ref[n_next]
            h_next = lax.shift_right_logical(n_next, 1) & (B_HEADS - 1)
            slope = slope_ref[h_next]
            dist0 = dist_pad - lax.shift_right_logical(n_next, 3) * tq
        for c in range(n_chunks):
            rows = slice(c * KEY_CHUNK, (c + 1) * KEY_CHUNK)
            if n_next is not None:
                drows = pl.ds(pl.multiple_of(dist0 + c * KEY_CHUNK, KEY_CHUNK), KEY_CHUNK)
                sc = _dot_nt(k_ref[0, h_next, rows, :], qm) + dist_ref[drows, :] * slope
                s_bufs[1 - par][rows, :] = sc
                m8_next = _col_max(sc, m8_next)
            if m8 is not None:
                p = jnp.exp2(s_bufs[par][rows, :] - m).astype(BF16)
                acc = acc + _dot(vt_ref[h_fin, c], p)
        if m8 is not None:
            (o1_ref, o2_ref)[par][...] = acc[0:B_V_DIM] * (1.0 / acc[B_V_DIM:B_V_DIM + 1])
        return m8_next

    def emit_head(n):
        h = lax.shift_right_logical(n, 1) & (B_HEADS - 1)
        rows = pl.ds(pl.multiple_of(lax.shift_right_logical(n, 3) * tq, tq), tq)
        o = o1_ref[...] - lam * o2_ref[...]
        o = o * lax.rsqrt(jnp.mean(o * o, axis=0, keepdims=True) + EPS)
        o = o.T * (gsub_ref[...] * (1.0 - LAM_INIT))
        o_ref[0, h, rows, :] = (o * g_ref[0, h, rows, :].astype(F32)).astype(BF16)

    def even_stage(n, m8):
        emit_head(n - 1)
        return run_stage(0, m8, n, n + 1)

    def odd_stage(n, m8):
        return run_stage(1, m8, n, n + 1)

    def stage(n, m8):
        return lax.cond((n & 1) == 0, functools.partial(even_stage, n),
                        functools.partial(odd_stage, n), m8)

    m8 = run_stage(1, None, None, jnp.int32(0))
    m8 = run_stage(0, m8, jnp.int32(0), jnp.int32(1))
    m8 = lax.fori_loop(1, n_units - 1, stage, m8)
    run_stage(1, m8, jnp.int32(n_units - 1), None)
    emit_head(jnp.int32(n_units - 1))


def _attn_b(qb, kb, vb, gb, lqk, gsub):
    bn, _, s, _ = qb.shape
    full = lambda b: (b, 0, 0, 0)
    const = lambda b: (0, 0)
    slopes = jnp.asarray([-sl * LOG2E for sl in ALIBI_SLOPES], F32)
    return pl.pallas_call(
        _attn_b_kernel,
        grid=(bn,),
        in_specs=[pl.BlockSpec(memory_space=pltpu.SMEM),
                  pl.BlockSpec((1, B_HEADS, s, LANES), full),
                  pl.BlockSpec((1, B_HEADS, s, LANES), full),
                  pl.BlockSpec((1, B_HEADS, s, LANES), full),
                  pl.BlockSpec((1, B_HEADS, s, LANES), full),
                  pl.BlockSpec((4, HEAD_DIM), const),
                  pl.BlockSpec((1, B_V_DIM), const)],
        out_specs=pl.BlockSpec((1, B_HEADS, s, LANES), full),
        out_shape=jax.ShapeDtypeStruct((bn, B_HEADS, s, LANES), BF16),
        scratch_shapes=[pltpu.VMEM((B_HEADS, s // KEY_CHUNK, B_V_DIM + ONES_ROWS, KEY_CHUNK), BF16),
                        pltpu.VMEM((2 * B_HEADS * (s // TQ_B), TQ_B, LANES), BF16),
                        pltpu.VMEM((s, TQ_B), F32),
                        pltpu.VMEM((s, TQ_B), F32),
                        pltpu.VMEM((2 * s - TQ_B, TQ_B), F32),
                        pltpu.VMEM((B_V_DIM, TQ_B), F32),
                        pltpu.VMEM((B_V_DIM, TQ_B), F32)],
        compiler_params=pltpu.CompilerParams(
            dimension_semantics=("arbitrary",), vmem_limit_bytes=VMEM_LIMIT),
        name="attn_b",
    )(slopes, qb, kb, vb, gb, lqk, gsub)


def _out_proj_kernel(x_ref, ada_ref, za_ref, zb_ref, ma_ref, mb_ref,
                     pa_ref, pb_ref, wo_ref, fg_ref, y_ref):
    gate = ada_ref[0][2:3, :]
    za = jnp.concatenate([za_ref[0, j] for j in range(N_SLABS)], axis=1)
    zb = jnp.concatenate([zb_ref[0, j] for j in range(N_SLABS)], axis=1)
    ya = _dot(za, pa_ref[...])
    yb = _dot(zb, pb_ref[...])
    merged = ma_ref[0].astype(F32) * ya + mb_ref[0].astype(F32) * yb
    out = x_ref[0] + gate * _dot(merged.astype(BF16), wo_ref[...])
    r = lax.rsqrt(jnp.mean(out * out, axis=-1, keepdims=True) + EPS)
    y_ref[0] = out * r * fg_ref[...]


def _out_proj(x, ada3, za, zb, ma, mb, pa, pb, wo, fg):
    bn, s, _ = x.shape
    const = lambda b, i: (0, 0)
    row = lambda b, i: (b, i, 0)
    return pl.pallas_call(
        _out_proj_kernel,
        grid=(bn, s // TM),
        in_specs=[pl.BlockSpec((1, TM, D_MODEL), row),
                  pl.BlockSpec((1, 3, D_MODEL), lambda b, i: (b, 0, 0)),
                  pl.BlockSpec((1, N_SLABS, TM, LANES), lambda b, i: (b, 0, i, 0)),
                  pl.BlockSpec((1, N_SLABS, TM, LANES), lambda b, i: (b, 0, i, 0)),
                  pl.BlockSpec((1, TM, D_MODEL), row),
                  pl.BlockSpec((1, TM, D_MODEL), row),
                  pl.BlockSpec((A_WIDTH, D_MODEL), const),
                  pl.BlockSpec((B_WIDTH, D_MODEL), const),
                  pl.BlockSpec((D_MODEL, D_MODEL), const),
                  pl.BlockSpec((1, D_MODEL), const)],
        out_specs=pl.BlockSpec((1, TM, D_MODEL), row),
        out_shape=jax.ShapeDtypeStruct((bn, s, D_MODEL), F32),
        compiler_params=pltpu.CompilerParams(
            dimension_semantics=("parallel", "arbitrary"), vmem_limit_bytes=VMEM_LIMIT),
        name="out_proj",
    )(x, ada3, za, zb, ma, mb, pa, pb, wo, fg)


def _rope_tables(s):
    rows = s // GRID_W
    row = jnp.repeat(jnp.arange(rows), GRID_W).astype(F32)
    col = jnp.tile(jnp.arange(GRID_W), rows).astype(F32)
    axis_dim = HEAD_DIM // 2
    inv_freq = ROPE_THETA ** (-jnp.arange(0, axis_dim, 2, dtype=F32) / axis_dim)
    ang_r = row[:, None] * inv_freq
    ang_c = col[:, None] * inv_freq
    cos = jnp.concatenate([jnp.cos(ang_r)] * 2 + [jnp.cos(ang_c)] * 2, axis=-1)
    sin = jnp.concatenate([-jnp.sin(ang_r), jnp.sin(ang_r),
                           -jnp.sin(ang_c), jnp.sin(ang_c)], axis=-1)
    return jnp.tile(cos, (1, 2)), jnp.tile(sin, (1, 2))


def _pair_perm():
    idx = []
    for j in range(A_GROUP):
        for head in (j, A_GROUP + j):
            idx.extend(range(head * HEAD_DIM, (head + 1) * HEAD_DIM))
    return jnp.asarray(idx, jnp.int32)


def _prep_weights(w_in, p_a, p_b, w_out):
    bounds = [0]
    for wdt in (A_WIDTH, A_KV_WIDTH, A_KV_WIDTH, A_WIDTH, B_WIDTH, B_WIDTH, B_WIDTH, B_WIDTH,
                D_MODEL, D_MODEL):
        bounds.append(bounds[-1] + wdt)
    parts = [w_in[:, bounds[i]:bounds[i + 1]] for i in range(10)]
    perm = _pair_perm()
    parts[0] = parts[0][:, perm]
    parts[3] = parts[3][:, perm]
    w_all = jnp.concatenate(parts, axis=1).astype(BF16)
    return w_all, p_a[perm, :].astype(BF16), p_b.astype(BF16), w_out.astype(BF16)


def _trunk(x, c, w_ada, b_ada, norm_g, w_all, gq, gk, lqk, gsub, pa, pb, wo, fg):
    bn = x.shape[0]
    cos, sin = _rope_tables(x.shape[1])
    ada3 = _ada(c, w_ada, b_ada).reshape(bn, 3, D_MODEL)
    qa, ka, va, ga, qb, kb, vb, gb, ma, mb = _in_proj(x, ada3, norm_g, w_all, gq, gk, cos, sin)
    za = _attn_a(qa, ka, va, ga)
    zb = _attn_b(qb, kb, vb, gb, lqk, gsub)
    return _out_proj(x, ada3, za, zb, ma, mb, pa, pb, wo, fg)


def kernel(x_prompt, x_sample, c_prompt, c_sample, w_ada, b_ada, norm_g, w_in, a_q_norm, a_k_norm,
           b_lq1, b_lk1, b_lq2, b_lk2, b_sub_norm, p_a, p_b, w_out, final_g):
    w_all, pa, pb, wo = _prep_weights(w_in[0], p_a[0], p_b[0], w_out[0])
    gq = jnp.tile(a_q_norm[0], 2).reshape(1, LANES)
    gk = jnp.tile(a_k_norm[0], 2).reshape(1, LANES)
    lqk = jnp.stack([b_lq1[0], b_lk1[0], b_lq2[0], b_lk2[0]]).astype(F32)
    gsub = b_sub_norm[0].reshape(1, B_V_DIM)
    fg = final_g.reshape(1, D_MODEL)
    args = (w_ada[0], b_ada[0], norm_g, w_all, gq, gk, lqk, gsub, pa, pb, wo, fg)
    y_prompt = _trunk(x_prompt, c_prompt, *args)
    y_sample = _trunk(x_sample, c_sample, *args)
    return (y_prompt, y_sample)
```
